```python
import math
import jax, jax.numpy as jnp
from jax import lax
import numpy as np

D_MODEL = 1024
BATCH = 32
SEQ = 2048
DEPTH = 4
DEC_BATCH = 16
DEC_SEQ = 32
PAST_LEN = 4096

CHUNK = 64
KEY_BLOCK = 16
N_AB = (DEPTH + 1) // 2
N_C = DEPTH // 2
D_A = D_MODEL
P_A = 64
H_A = D_A // P_A
N_A = 128
G_A = 2
CONV_K = 4
CONV_DIM_A = D_A + 2 * G_A * N_A
H_B = 4
DK_B = D_MODEL // 2 // H_B
DV_B = D_MODEL // H_B
GLA_LOWRANK = 16
GLA_GATE_NORM = 16.0
HGRN_EXPAND = 128
H_C = D_MODEL // HGRN_EXPAND
DK_C = HGRN_EXPAND
DV_C = D_MODEL // H_C
D_C = H_C * DK_C
D_FF = 2816
EPS = 1e-6
TINY = 1e-30
IN_AB = D_A + CONV_DIM_A + H_A + 2 * H_B * DK_B + 2 * H_B * DV_B + GLA_LOWRANK
IN_C = 2 * D_C + 2 * H_C * DV_C
F32 = jnp.float32

kernel_name = 'hybrid_ssd_gla_hgrn2_macaron_stream'


def pad_time(a, tp):
    return jnp.pad(a, [(0, 0), (0, tp - a.shape[1])] + [(0, 0)] * (a.ndim - 2))


def blocks(a, size):
    bsz, tp = a.shape[0], a.shape[1]
    return jnp.moveaxis(a.reshape((bsz, tp // size, size) + a.shape[2:]), 1, 0)


def unblocks(a):
    a = jnp.moveaxis(a, 0, 1)
    return a.reshape((a.shape[0], a.shape[1] * a.shape[2]) + a.shape[3:])


def split_cols(a, sizes):
    return jnp.split(a, np.cumsum(sizes)[:-1].tolist(), axis=-1)


def masked_decay(diff, mask):
    return jnp.where(mask, jnp.exp(jnp.where(mask, diff, 0.0)), 0.0)


def rms_norm(x, w):
    xf = x.astype(F32)
    y = xf * lax.rsqrt(jnp.mean(xf * xf, axis=-1, keepdims=True) + EPS)
    return (y * w.astype(F32)).astype(x.dtype)


def swiglu(x, w_gu, w_down):
    gate, up = jnp.split(x @ w_gu, 2, axis=-1)
    return (jax.nn.silu(gate) * up) @ w_down


def causal_dwconv(u, buf, w, b):
    ext = jnp.concatenate([buf.astype(u.dtype), u], axis=1)
    out = lax.conv_general_dilated(ext, w[:, None, :].astype(u.dtype), (1,), 'VALID',
                                   dimension_numbers=('NWC', 'WIO', 'NWC'),
                                   feature_group_count=u.shape[-1])
    return out + b, ext[:, -(CONV_K - 1):]


def ssd_scan(x, dt, a, bm, cm, h0):
    bsz, t, nh, hp = x.shape
    ng, ns = bm.shape[2], bm.shape[3]
    hpg = nh // ng
    tp = -(-t // CHUNK) * CHUNK
    xs = (blocks(pad_time(x, tp).reshape(bsz, tp, ng, hpg, hp), CHUNK),
          blocks(pad_time(dt, tp).reshape(bsz, tp, ng, hpg), CHUNK),
          blocks(pad_time(bm, tp), CHUNK),
          blocks(pad_time(cm, tp), CHUNK))
    a = a.reshape(ng, hpg)
    mask = jnp.tril(jnp.ones((CHUNK, CHUNK), dtype=bool))[None, :, :, None, None]

    def step(h, blk):
        xb, dtb, bb, cb = blk
        cum = jnp.cumsum(dtb * a, axis=1)
        decay = masked_decay(cum[:, :, None] - cum[:, None, :], mask)
        xdt = xb * dtb[..., None]
        cbs = jnp.einsum('bign,bjgn->bijg', cb, bb)
        y = jnp.einsum('bijg,bijgh,bjghp->bighp', cbs, decay, xdt)
        y = y + jnp.einsum('bign,bghpn,bigh->bighp', cb, h, jnp.exp(cum))
        h_new = (h * jnp.exp(cum[:, -1])[..., None, None]
                 + jnp.einsum('bjgn,bjgh,bjghp->bghpn', bb, jnp.exp(cum[:, -1:] - cum), xdt))
        return h_new, y

    h_t, ys = lax.scan(step, h0.reshape(bsz, ng, hpg, hp, ns), xs)
    return unblocks(ys).reshape(bsz, tp, nh, hp)[:, :t], h_t.reshape(bsz, nh, hp, ns)


def gated_la_scan(q, k, v, log_g, s0):
    t = q.shape[1]
    tp = -(-t // KEY_BLOCK) * KEY_BLOCK
    xs = tuple(blocks(pad_time(a, tp), KEY_BLOCK) for a in (q, k, v, log_g))
    mask = jnp.tril(jnp.ones((KEY_BLOCK, KEY_BLOCK), dtype=bool))[None, :, :, None, None]

    def step(s, blk):
        qb, kb, vb, gb = blk
        b = jnp.cumsum(gb, axis=1)
        decay = masked_decay(b[:, :, None] - b[:, None, :], mask)
        att = jnp.einsum('bihk,bijhk,bjhk->bijh', qb, decay, kb)
        o = (jnp.einsum('bijh,bjhv->bihv', att, vb)
             + jnp.einsum('bihk,bhkv->bihv', qb * jnp.exp(b), s))
        b_end = b[:, -1]
        s_new = (s * jnp.exp(b_end)[..., None]
                 + jnp.einsum('bjhk,bjhv->bhkv', kb * jnp.exp(b_end[:, None] - b), vb))
        return s_new, o

    s_t, outs = lax.scan(step, s0, xs)
    return unblocks(outs)[:, :t], s_t


def mixer_ab(u, ssm_h, conv_buf, gla_s, p, j):
    bsz, t, _ = u.shape
    z, xbc, dt, q, k, v, g, glr = split_cols(
        u @ p['ab_w_in'][j],
        [D_A, CONV_DIM_A, H_A, H_B * DK_B, H_B * DK_B, H_B * DV_B, H_B * DV_B, GLA_LOWRANK])
    xbc, conv_new = causal_dwconv(xbc, conv_buf, p['ssm_conv_w'][j], p['ssm_conv_b'][j])
    xbc = jax.nn.silu(xbc).astype(F32)
    xs, bm, cm = split_cols(xbc, [D_A, G_A * N_A, G_A * N_A])
    xs = xs.reshape(bsz, t, H_A, P_A)
    dt = jax.nn.softplus(dt.astype(F32) + p['ssm_dt_bias'][j])
    a = -jnp.exp(p['ssm_a_log'][j].astype(F32))
    y, ssm_new = ssd_scan(xs, dt, a, bm.reshape(bsz, t, G_A, N_A), cm.reshape(bsz, t, G_A, N_A),
                          ssm_h.astype(F32))
    y = (y + p['ssm_d'][j][:, None] * xs).reshape(bsz, t, D_A) * jax.nn.silu(z.astype(F32))
    y = rms_norm(y.reshape(bsz, t, G_A, D_A // G_A),
                 p['ssm_norm_w'][j].reshape(G_A, D_A // G_A)).reshape(bsz, t, D_A)
    q = q.astype(F32).reshape(bsz, t, H_B, DK_B) * DK_B ** -0.5
    k = k.astype(F32).reshape(bsz, t, H_B, DK_B)
    v = v.astype(F32).reshape(bsz, t, H_B, DV_B)
    gk = jax.nn.log_sigmoid((glr @ p['gla_w_gk'][j] + p['gla_b_gk'][j]).astype(F32)) / GLA_GATE_NORM
    o, gla_new = gated_la_scan(q, k, v, gk.reshape(bsz, t, H_B, DK_B), gla_s.astype(F32))
    o = rms_norm(o, p['gla_norm_w'][j]).reshape(bsz, t, H_B * DV_B) * jax.nn.silu(g.astype(F32))
    out = jnp.concatenate([y, o], axis=-1).astype(u.dtype) @ p['ab_w_out'][j]
    return out, ssm_new.astype(ssm_h.dtype), conv_new, gla_new.astype(gla_s.dtype)


def hgrn_lower_bounds(logits):
    s = jax.nn.softmax(logits.astype(F32), axis=0)
    return jnp.cumsum(s, axis=0) - s[0]


def mixer_c(u, s0, p, j, lb):
    bsz, t, _ = u.shape
    q, f, i, g = split_cols(u @ p['c_w_in'][j], [D_C, D_C, H_C * DV_C, H_C * DV_C])
    f = f.astype(F32)
    sig = jax.nn.sigmoid(f)
    forget = lb + (1.0 - lb) * sig
    log_f = jnp.log(jnp.maximum(forget, TINY))
    k = 1.0 - forget
    q = jax.nn.silu(q.astype(F32))
    o, s_new = gated_la_scan(q.reshape(bsz, t, H_C, DK_C), k.reshape(bsz, t, H_C, DK_C),
                             i.astype(F32).reshape(bsz, t, H_C, DV_C),
                             log_f.reshape(bsz, t, H_C, DK_C), s0.astype(F32))
    o = rms_norm(o, p['hgrn_norm_w'][j]).reshape(bsz, t, H_C * DV_C) * jax.nn.silu(g.astype(F32))
    return o.astype(u.dtype) @ p['c_w_out'][j], s_new.astype(s0.dtype)


def trunk(x, ssm0, conv0, gla0, hgrn0, p):
    lbs = hgrn_lower_bounds(p['hgrn_lb_logits'])
    new_ssm, new_conv, new_gla, new_hgrn = [], [], [], []
    h = x
    for l in range(DEPTH):
        h = h + 0.5 * swiglu(rms_norm(h, p['norm_ffn1'][l]), p['ffn1_w_gu'][l], p['ffn1_w_down'][l])
        u = rms_norm(h, p['norm_mix'][l])
        j = l // 2
        if l % 2 == 0:
            m, s_ssm, s_conv, s_gla = mixer_ab(u, ssm0[j], conv0[j], gla0[j], p, j)
            new_ssm.append(s_ssm)
            new_conv.append(s_conv)
            new_gla.append(s_gla)
        else:
            m, s_h = mixer_c(u, hgrn0[j], p, j, lbs[j])
            new_hgrn.append(s_h)
        h = h + m
        h = h + 0.5 * swiglu(rms_norm(h, p['norm_ffn2'][l]), p['ffn2_w_gu'][l], p['ffn2_w_down'][l])
    y = rms_norm(h, p['norm_final'])
    return y, jnp.stack(new_ssm), jnp.stack(new_conv), jnp.stack(new_gla), jnp.stack(new_hgrn)


def setup_inputs(seed: int = 0) -> dict:
    key = jax.random.key(seed)
    keys = list(jax.random.split(key, 40))

    def nrm(shape, scale):
        return scale * jax.random.normal(keys.pop(), shape, F32)

    def gain(shape):
        return 1.0 + 0.02 * jax.random.normal(keys.pop(), shape, F32)

    dt0 = jnp.exp(jax.random.uniform(keys.pop(), (N_AB, H_A), F32, math.log(1e-3), math.log(1e-1)))
    a_log = jnp.log(jax.random.uniform(keys.pop(), (N_AB, H_A), F32, 1.0, 16.0))
    w_out_ab = D_A + H_B * DV_B
    return {
        'x_prompt': nrm((BATCH, SEQ, D_MODEL), 1.0),
        'x_sample': nrm((DEC_BATCH, DEC_SEQ, D_MODEL), 1.0),
        'state_ssm': nrm((N_AB, DEC_BATCH, H_A, P_A, N_A), 0.05),
        'state_conv': nrm((N_AB, DEC_BATCH, CONV_K - 1, CONV_DIM_A), 1.0),
        'state_gla': nrm((N_AB, DEC_BATCH, H_B, DK_B, DV_B), 0.5),
        'state_hgrn': nrm((N_C, DEC_BATCH, H_C, DK_C, DV_C), 0.5),
        'norm_ffn1': gain((DEPTH, D_MODEL)),
        'ffn1_w_gu': nrm((DEPTH, D_MODEL, 2 * D_FF), D_MODEL ** -0.5),
        'ffn1_w_down': nrm((DEPTH, D_FF, D_MODEL), D_FF ** -0.5),
        'norm_mix': gain((DEPTH, D_MODEL)),
        'ab_w_in': nrm((N_AB, D_MODEL, IN_AB), D_MODEL ** -0.5),
        'ssm_conv_w': nrm((N_AB, CONV_K, CONV_DIM_A), CONV_K ** -0.5),
        'ssm_conv_b': nrm((N_AB, CONV_DIM_A), 0.02),
        'ssm_dt_bias': dt0 + jnp.log(-jnp.expm1(-dt0)),
        'ssm_a_log': a_log,
        'ssm_d': gain((N_AB, H_A)),
        'ssm_norm_w': gain((N_AB, D_A)),
        'gla_w_gk': nrm((N_AB, GLA_LOWRANK, H_B * DK_B), GLA_LOWRANK ** -0.5),
        'gla_b_gk': nrm((N_AB, H_B * DK_B), 0.02),
        'gla_norm_w': gain((N_AB, DV_B)),
        'ab_w_out': nrm((N_AB, w_out_ab, D_MODEL), w_out_ab ** -0.5),
        'c_w_in': nrm((N_C, D_MODEL, IN_C), D_MODEL ** -0.5),
        'hgrn_lb_logits': nrm((N_C, D_C), 1.0),
        'hgrn_norm_w': gain((N_C, DV_C)),
        'c_w_out': nrm((N_C, H_C * DV_C, D_MODEL), (H_C * DV_C) ** -0.5),
        'norm_ffn2': gain((DEPTH, D_MODEL)),
        'ffn2_w_gu': nrm((DEPTH, D_MODEL, 2 * D_FF), D_MODEL ** -0.5),
        'ffn2_w_down': nrm((DEPTH, D_FF, D_MODEL), D_FF ** -0.5),
        'norm_final': gain((D_MODEL,)),
    }


def reference(x_prompt, x_sample, state_ssm, state_conv, state_gla, state_hgrn,
              norm_ffn1, ffn1_w_gu, ffn1_w_down, norm_mix, ab_w_in, ssm_conv_w, ssm_conv_b,
              ssm_dt_bias, ssm_a_log, ssm_d, ssm_norm_w, gla_w_gk, gla_b_gk, gla_norm_w, ab_w_out,
              c_w_in, hgrn_lb_logits, hgrn_norm_w, c_w_out, norm_ffn2, ffn2_w_gu, ffn2_w_down,
              norm_final):
    p = dict(norm_ffn1=norm_ffn1, ffn1_w_gu=ffn1_w_gu, ffn1_w_down=ffn1_w_down, norm_mix=norm_mix,
             ab_w_in=ab_w_in, ssm_conv_w=ssm_conv_w, ssm_conv_b=ssm_conv_b, ssm_dt_bias=ssm_dt_bias,
             ssm_a_log=ssm_a_log, ssm_d=ssm_d, ssm_norm_w=ssm_norm_w, gla_w_gk=gla_w_gk,
             gla_b_gk=gla_b_gk, gla_norm_w=gla_norm_w, ab_w_out=ab_w_out, c_w_in=c_w_in,
             hgrn_lb_logits=hgrn_lb_logits, hgrn_norm_w=hgrn_norm_w, c_w_out=c_w_out,
             norm_ffn2=norm_ffn2, ffn2_w_gu=ffn2_w_gu, ffn2_w_down=ffn2_w_down, norm_final=norm_final)
    bp = x_prompt.shape[0]
    dtp = x_prompt.dtype
    y_prompt, ssm_p, conv_p, gla_p, hgrn_p = trunk(
        x_prompt,
        jnp.zeros((N_AB, bp, H_A, P_A, N_A), dtp),
        jnp.zeros((N_AB, bp, CONV_K - 1, CONV_DIM_A), dtp),
        jnp.zeros((N_AB, bp, H_B, DK_B, DV_B), dtp),
        jnp.zeros((N_C, bp, H_C, DK_C, DV_C), dtp),
        p)
    y_sample, ssm_s, conv_s, gla_s, hgrn_s = trunk(x_sample, state_ssm, state_conv, state_gla,
                                                   state_hgrn, p)
    return (y_prompt, y_sample, ssm_p, conv_p, gla_p, hgrn_p, ssm_s, conv_s, gla_s, hgrn_s)
```

```python
import functools

import jax
import jax.numpy as jnp
from jax import lax
from jax.experimental import pallas as pl
from jax.experimental.pallas import tpu as pltpu

F32 = jnp.float32
BF16 = jnp.bfloat16
HIGHEST = lax.Precision.HIGHEST

EPS = 1e-6
TINY = 1e-30
CONV_K = 4
P_A = 64
N_A = 128
G_A = 2
GLA_GATE_NORM = 16.0
LANES = 128
SUBLANES = 8
HIST_ROW = SUBLANES - (CONV_K - 1)

SCAN_CHUNK = 128
MIXER_TILE = 256
FFN_TILE = 512
FFN_COLS = 512
VMEM_LIMIT = 56 * 1024 * 1024


def _mm(a, b):
    return jnp.dot(a, b, preferred_element_type=F32)


def _mm_nt(a, b):
    return lax.dot_general(a, b, (((1,), (1,)), ((), ())), preferred_element_type=F32)


def _mm_tn(a, b):
    return lax.dot_general(a, b, (((0,), (0,)), ((), ())), preferred_element_type=F32)


def _mm_exact(a, b):
    return jnp.dot(a, b, precision=HIGHEST, preferred_element_type=F32)


def _rms(x, w):
    return x * lax.rsqrt(jnp.mean(x * x, axis=-1, keepdims=True) + EPS) * w


def _silu(x):
    return x * jax.nn.sigmoid(x)


def _softplus(x):
    return jnp.maximum(x, 0.0) + jnp.log1p(jnp.exp(-jnp.abs(x)))


def _resident(shape):
    nd = len(shape)
    return pl.BlockSpec(shape, lambda *_: (0,) * nd, pipeline_mode=pl.Buffered(1))


def _ffn_kernel(*refs, d_ff, final_norm):
    if final_norm:
        x_ref, nw_ref, wgu_ref, wd_ref, fw_ref, o_ref, act_ref = refs
    else:
        x_ref, nw_ref, wgu_ref, wd_ref, o_ref, act_ref = refs
    x = x_ref[...]
    xn = _rms(x, nw_ref[...]).astype(BF16)
    for c0 in range(0, d_ff, FFN_COLS):
        cw = min(FFN_COLS, d_ff - c0)
        gate = _mm(xn, wgu_ref[:, c0:c0 + cw])
        up = _mm(xn, wgu_ref[:, d_ff + c0:d_ff + c0 + cw])
        act_ref[:, c0:c0 + cw] = (_silu(gate) * up).astype(BF16)
    y = x + 0.5 * _mm(act_ref[...], wd_ref[...])
    if final_norm:
        y = _rms(y, fw_ref[...])
    o_ref[...] = y


def _ffn(x2d, nw, wgu, wd, final_w=None):
    m, d = x2d.shape
    d_ff = wd.shape[0]
    tm = min(FFN_TILE, m)
    assert m % tm == 0
    final_norm = final_w is not None
    in_specs = [pl.BlockSpec((tm, d), lambda i: (i, 0)), _resident((1, d)),
                _resident(wgu.shape), _resident(wd.shape)]
    args = [x2d, nw, wgu, wd]
    if final_norm:
        in_specs.append(_resident((1, d)))
        args.append(final_w)
    return pl.pallas_call(
        functools.partial(_ffn_kernel, d_ff=d_ff, final_norm=final_norm),
        grid=(m // tm,),
        in_specs=in_specs,
        out_specs=pl.BlockSpec((tm, d), lambda i: (i, 0)),
        out_shape=jax.ShapeDtypeStruct((m, d), F32),
        scratch_shapes=[pltpu.VMEM((tm, d_ff), BF16)],
        compiler_params=pltpu.CompilerParams(dimension_semantics=("parallel",),
                                             vmem_limit_bytes=VMEM_LIMIT),
        name="ffn_final" if final_norm else "ffn",
    )(*args)


def _chunk_masks(length):
    i = lax.broadcasted_iota(jnp.int32, (length, length), 0)
    j = lax.broadcasted_iota(jnp.int32, (length, length), 1)
    return i >= j, i == j, jnp.where(i > j, i ^ j, 0)


def _gated_chunk(q, k, v, lg, st_ref, head, eye, split):
    length, kdim = q.shape
    row = lax.broadcasted_iota(jnp.int32, (length, kdim), 0)
    qb = q.astype(BF16)
    kb = k.astype(BF16)
    att = jnp.where(eye, _mm_nt(qb, kb), 0.0)
    pre = lg
    tot = lg
    s = 1
    while s < length:
        qs = (q * jnp.exp(pre)).astype(BF16)
        ks = (k * jnp.exp(tot - pre)).astype(BF16)
        att = jnp.where(split >= s, _mm_nt(qs, ks), att)
        upper = (row & s) != 0
        below = pltpu.roll(tot, s, 0)
        above = pltpu.roll(tot, length - s, 0)
        pre = pre + jnp.where(upper, below, 0.0)
        tot = tot + jnp.where(upper, below, above)
        s *= 2
    st = st_ref[head]
    o = _mm(att.astype(BF16), v.astype(BF16)) + _mm_nt((q * jnp.exp(pre)).astype(BF16),
                                                        st.astype(BF16))
    kend = (k * jnp.exp(tot - pre)).astype(BF16)
    st_ref[head] = st * jnp.exp(tot[0:1, :]) + _mm_tn(v.astype(BF16), kend)
    return o


def _head_norm_gate(o, w, gate):
    return _rms(o, w) * _silu(gate)


def _row_valid(shape, tv):
    return lax.broadcasted_iota(jnp.int32, shape, 0) < tv


def _ssd_chunk(xs, bm, cm, dt, a_row, rexp, st_ref, causal):
    length = xs.shape[0]
    heads = xs.shape[1] // P_A
    pair_w = 2 * P_A
    pairs_per_group = heads // 2 // G_A
    tril = causal.astype(F32)
    cum = _mm_exact(tril, dt * a_row)
    cum_t = cum.T
    cum_x = _mm_exact(cum, rexp)
    xdt = xs * _mm_exact(dt, rexp)
    last = cum_x[length - 1:length, :]
    from_start = jnp.exp(cum_x)
    to_end = (xdt * jnp.exp(last - cum_x)).astype(BF16)
    end_decay = jnp.exp(last)
    xdt = xdt.astype(BF16)
    lane = lax.broadcasted_iota(jnp.int32, (length, pair_w), 1)
    ys = []
    for g in range(G_A):
        bg = bm[:, g * N_A:(g + 1) * N_A].astype(BF16)
        cg = cm[:, g * N_A:(g + 1) * N_A].astype(BF16)
        cbs = _mm_nt(cg, bg)
        for pr in range(pairs_per_group):
            hp = g * pairs_per_group + pr
            sl = slice(hp * pair_w, (hp + 1) * pair_w)
            xp = xdt[:, sl]
            y = None
            for half in range(2):
                hh = 2 * hp + half
                diff = cum[:, hh:hh + 1] - cum_t[hh:hh + 1, :]
                decay = jnp.exp(jnp.where(causal, diff, -1e30))
                m = (cbs * decay).astype(BF16)
                own = (lane >= P_A) if half else (lane < P_A)
                part = _mm(m, jnp.where(own, xp, jnp.zeros_like(xp)))
                y = part if y is None else y + part
            st = st_ref[:, sl]
            ys.append(y + _mm(cg, st.astype(BF16)) * from_start[:, sl])
            st_ref[:, sl] = st * end_decay[:, sl] + _mm_tn(bg, to_end[:, sl])
    return jnp.concatenate(ys, axis=1)


def _mixer_ab_kernel(*refs, tt_in, tt, tv, has_state):
    (h_ref, nw_ref, wm_ref, ws_ref, cw_ref, cb_ref, dtb_ref, alog_ref, dexp_ref, snw_ref,
     wgk_ref, bgk_ref, gnw_ref, wo_ref, rexp_ref) = refs[:15]
    refs = refs[15:]
    if has_state:
        ssm0_ref, conv0_ref, gla0_ref = refs[:3]
        refs = refs[3:]
    ho_ref, ssm_o_ref, conv_o_ref, gla_o_ref, ext_ref, ssm_ref, gla_ref = refs

    d_a = dexp_ref.shape[1]
    conv_dim = cw_ref.shape[1]
    n_heads_b, dv_b, dk_b = gla_ref.shape
    qk_w = n_heads_b * dk_b
    v_w = n_heads_b * dv_b
    n_pairs = d_a // (2 * P_A)
    c_xbc, c_q, c_k, c_v, c_g = d_a, d_a + conv_dim, d_a + conv_dim + qk_w, \
        d_a + conv_dim + 2 * qk_w, d_a + conv_dim + 2 * qk_w + v_w
    t = pl.program_id(1)
    last_t = pl.num_programs(1) - 1

    @pl.when(t == 0)
    def _():
        if has_state:
            for hp in range(n_pairs):
                sl = slice(hp * 2 * P_A, (hp + 1) * 2 * P_A)
                ssm_ref[:, sl] = ssm0_ref[sl, :].T
            ext_ref[HIST_ROW:SUBLANES, :] = conv0_ref[...]
            for hd in range(n_heads_b):
                gla_ref[hd] = gla0_ref[hd].T
        else:
            ssm_ref[...] = jnp.zeros_like(ssm_ref)
            ext_ref[HIST_ROW:SUBLANES, :] = jnp.zeros((CONV_K - 1, conv_dim), F32)
            gla_ref[...] = jnp.zeros_like(gla_ref)

    h = h_ref[...]
    xn = _rms(h, nw_ref[...]).astype(BF16)
    if tt_in < tt:
        xn = jnp.concatenate([xn, jnp.zeros((tt - tt_in, xn.shape[1]), BF16)], axis=0)
    causal, eye, split = _chunk_masks(SCAN_CHUNK)

    ext_ref[SUBLANES:SUBLANES + tt, :] = _mm(xn, wm_ref[:, c_xbc:c_q])
    conv = cb_ref[...]
    for kk in range(CONV_K):
        conv = conv + cw_ref[kk:kk + 1, :] * ext_ref[HIST_ROW + kk:HIST_ROW + kk + tt, :]
    hist = ext_ref[HIST_ROW + tv:SUBLANES + tv, :]
    ext_ref[HIST_ROW:SUBLANES, :] = hist

    @pl.when(t == last_t)
    def _():
        conv_o_ref[...] = hist

    xbc = _silu(conv)
    xs = xbc[:, :d_a]
    bm = xbc[:, d_a:d_a + G_A * N_A]
    cm = xbc[:, d_a + G_A * N_A:]
    small = _mm(xn, ws_ref[...])
    dt = _softplus(small + dtb_ref[...])
    if tv < tt:
        dt = jnp.where(_row_valid(dt.shape, tv), dt, 0.0)
    lane = lax.broadcasted_iota(jnp.int32, (1, LANES), 1)
    a_row = jnp.where(lane < d_a // P_A, -jnp.exp(alog_ref[...]), 0.0)
    rexp = rexp_ref[...]
    ys = []
    for c0 in range(0, tt, SCAN_CHUNK):
        rs = slice(c0, c0 + SCAN_CHUNK)
        ys.append(_ssd_chunk(xs[rs], bm[rs], cm[rs], dt[rs], a_row, rexp, ssm_ref, causal))
    y = jnp.concatenate(ys, axis=0) if len(ys) > 1 else ys[0]
    y = (y + dexp_ref[...] * xs) * _silu(_mm(xn, wm_ref[:, 0:c_xbc]))
    gw = d_a // G_A
    y = jnp.concatenate([_rms(y[:, g * gw:(g + 1) * gw], snw_ref[:, g * gw:(g + 1) * gw])
                         for g in range(G_A)], axis=1)
    acc = _mm(y.astype(BF16), wo_ref[0:d_a, :])

    q = _mm(xn, wm_ref[:, c_q:c_k]) * dk_b ** -0.5
    k = _mm(xn, wm_ref[:, c_k:c_v])
    v = _mm(xn, wm_ref[:, c_v:c_g])
    gate = _mm(xn, wm_ref[:, c_g:c_g + v_w])
    gk = _mm(small.astype(BF16), wgk_ref[...]) + bgk_ref[...]
    lg = -_softplus(-gk) / GLA_GATE_NORM
    if tv < tt:
        lg = jnp.where(_row_valid(lg.shape, tv), lg, 0.0)
        k = jnp.where(_row_valid(k.shape, tv), k, 0.0)
    rows = []
    for c0 in range(0, tt, SCAN_CHUNK):
        rs = slice(c0, c0 + SCAN_CHUNK)
        heads = []
        for hd in range(n_heads_b):
            ks = slice(hd * dk_b, (hd + 1) * dk_b)
            vs = slice(hd * dv_b, (hd + 1) * dv_b)
            o = _gated_chunk(q[rs, ks], k[rs, ks], v[rs, vs], lg[rs, ks], gla_ref, hd, eye, split)
            heads.append(_head_norm_gate(o, gnw_ref[...], gate[rs, vs]))
        rows.append(jnp.concatenate(heads, axis=1))
    o = jnp.concatenate(rows, axis=0) if len(rows) > 1 else rows[0]
    acc = acc + _mm(o.astype(BF16), wo_ref[d_a:d_a + v_w, :])
    ho_ref[...] = h + acc[:tt_in]

    @pl.when(t == last_t)
    def _():
        for hp in range(n_pairs):
            sl = slice(hp * 2 * P_A, (hp + 1) * 2 * P_A)
            ssm_o_ref[sl, :] = ssm_ref[:, sl].T
        for hd in range(n_heads_b):
            gla_o_ref[hd] = gla_ref[hd].T


def _tiles(t):
    if t % MIXER_TILE == 0:
        return MIXER_TILE, MIXER_TILE, MIXER_TILE
    assert t <= SCAN_CHUNK and t % (2 * SUBLANES) == 0
    return t, SCAN_CHUNK, t


def _batch_block(shape):
    nd = len(shape)
    return pl.BlockSpec((None,) + tuple(shape[1:]), lambda b, t: (b,) + (0,) * (nd - 1))


def _mixer_ab(h, w, states):
    bsz, t, d = h.shape
    tt_in, tt, tv = _tiles(t)
    has_state = states is not None
    d_a = w["dexp"].shape[1]
    conv_dim = w["conv_w"].shape[1]
    n_heads_b, dk_b, dv_b = w["gla_shape"]
    consts = [w["norm"], w["w_main"], w["w_small"], w["conv_w"], w["conv_b"], w["dt_bias"],
              w["a_log"], w["dexp"], w["ssm_norm"], w["w_gk"], w["b_gk"], w["gla_norm"],
              w["w_out"], w["rexp"]]
    args = [h] + consts
    in_specs = [pl.BlockSpec((None, tt_in, d), lambda b, i: (b, i, 0))]
    in_specs += [_resident(c.shape) for c in consts]
    if has_state:
        ssm0, conv0, gla0 = states
        ssm0 = ssm0.reshape(bsz, d_a, N_A)
        args += [ssm0, conv0, gla0]
        in_specs += [_batch_block(ssm0.shape), _batch_block(conv0.shape), _batch_block(gla0.shape)]
    out_shape = [jax.ShapeDtypeStruct((bsz, t, d), F32),
                 jax.ShapeDtypeStruct((bsz, d_a, N_A), F32),
                 jax.ShapeDtypeStruct((bsz, CONV_K - 1, conv_dim), F32),
                 jax.ShapeDtypeStruct((bsz, n_heads_b, dk_b, dv_b), F32)]
    out_specs = [pl.BlockSpec((None, tt_in, d), lambda b, i: (b, i, 0))]
    out_specs += [_batch_block(s.shape) for s in out_shape[1:]]
    ho, ssm, conv, gla = pl.pallas_call(
        functools.partial(_mixer_ab_kernel, tt_in=tt_in, tt=tt, tv=tv, has_state=has_state),
        grid=(bsz, t // tt_in),
        in_specs=in_specs,
        out_specs=out_specs,
        out_shape=out_shape,
        scratch_shapes=[pltpu.VMEM((SUBLANES + tt, conv_dim), F32),
                        pltpu.VMEM((N_A, d_a), F32),
                        pltpu.VMEM((n_heads_b, dv_b, dk_b), F32)],
        compiler_params=pltpu.CompilerParams(dimension_semantics=("parallel", "arbitrary"),
                                             vmem_limit_bytes=VMEM_LIMIT),
        name="mixer_ab_state" if has_state else "mixer_ab",
    )(*args)
    return ho, ssm.reshape(bsz, d_a // P_A, P_A, N_A), conv, gla


def _mixer_c_kernel(*refs, tt_in, tt, tv, layer, has_state):
    h_ref, nw_ref, wi_ref, lbl_ref, hnw_ref, wo_ref = refs[:6]
    refs = refs[6:]
    if has_state:
        st0_ref = refs[0]
        refs = refs[1:]
    ho_ref, st_o_ref, st_ref = refs
    n_heads, dv, dk = st_ref.shape
    d_c = n_heads * dk
    v_w = n_heads * dv
    t = pl.program_id(1)
    last_t = pl.num_programs(1) - 1

    @pl.when(t == 0)
    def _():
        if has_state:
            for hd in range(n_heads):
                st_ref[hd] = st0_ref[hd].T
        else:
            st_ref[...] = jnp.zeros_like(st_ref)

    logits = lbl_ref[...]
    e = jnp.exp(logits - jnp.max(logits, axis=0, keepdims=True))
    sm = e / jnp.sum(e, axis=0, keepdims=True)
    lb = jnp.sum(sm[0:layer + 1, :], axis=0, keepdims=True) - sm[0:1, :]

    h = h_ref[...]
    xn = _rms(h, nw_ref[...]).astype(BF16)
    if tt_in < tt:
        xn = jnp.concatenate([xn, jnp.zeros((tt - tt_in, xn.shape[1]), BF16)], axis=0)
    _, eye, split = _chunk_masks(SCAN_CHUNK)

    q = _silu(_mm(xn, wi_ref[:, 0:d_c]))
    forget = lb + (1.0 - lb) * jax.nn.sigmoid(_mm(xn, wi_ref[:, d_c:2 * d_c]))
    lg = jnp.log(jnp.maximum(forget, TINY))
    k = 1.0 - forget
    if tv < tt:
        lg = jnp.where(_row_valid(lg.shape, tv), lg, 0.0)
        k = jnp.where(_row_valid(k.shape, tv), k, 0.0)
    v = _mm(xn, wi_ref[:, 2 * d_c:2 * d_c + v_w])
    gate = _mm(xn, wi_ref[:, 2 * d_c + v_w:2 * d_c + 2 * v_w])
    rows = []
    for c0 in range(0, tt, SCAN_CHUNK):
        rs = slice(c0, c0 + SCAN_CHUNK)
        heads = []
        for hd in range(n_heads):
            ks = slice(hd * dk, (hd + 1) * dk)
            vs = slice(hd * dv, (hd + 1) * dv)
            o = _gated_chunk(q[rs, ks], k[rs, ks], v[rs, vs], lg[rs, ks], st_ref, hd, eye, split)
            heads.append(_head_norm_gate(o, hnw_ref[...], gate[rs, vs]))
        rows.append(jnp.concatenate(heads, axis=1))
    o = jnp.concatenate(rows, axis=0) if len(rows) > 1 else rows[0]
    ho_ref[...] = h + _mm(o.astype(BF16), wo_ref[...])[:tt_in]

    @pl.when(t == last_t)
    def _():
        for hd in range(n_heads):
            st_o_ref[hd] = st_ref[hd].T


def _mixer_c(h, w, layer, state):
    bsz, t, d = h.shape
    tt_in, tt, tv = _tiles(t)
    has_state = state is not None
    n_heads, dk, dv = w["shape"]
    consts = [w["norm"], w["w_in"], w["lb_logits"], w["hgrn_norm"], w["w_out"]]
    args = [h] + consts
    in_specs = [pl.BlockSpec((None, tt_in, d), lambda b, i: (b, i, 0))]
    in_specs += [_resident(c.shape) for c in consts]
    if has_state:
        args.append(state)
        in_specs.append(_batch_block(state.shape))
    out_shape = [jax.ShapeDtypeStruct((bsz, t, d), F32),
                 jax.ShapeDtypeStruct((bsz, n_heads, dk, dv), F32)]
    out_specs = [pl.BlockSpec((None, tt_in, d), lambda b, i: (b, i, 0)),
                 _batch_block(out_shape[1].shape)]
    return pl.pallas_call(
        functools.partial(_mixer_c_kernel, tt_in=tt_in, tt=tt, tv=tv, layer=layer,
                          has_state=has_state),
        grid=(bsz, t // tt_in),
        in_specs=in_specs,
        out_specs=out_specs,
        out_shape=out_shape,
        scratch_shapes=[pltpu.VMEM((n_heads, dv, dk), F32)],
        compiler_params=pltpu.CompilerParams(dimension_semantics=("parallel", "arbitrary"),
                                             vmem_limit_bytes=VMEM_LIMIT),
        name="mixer_c_state" if has_state else "mixer_c",
    )(*args)


def _pad_lanes(a):
    return jnp.pad(a, [(0, 0)] * (a.ndim - 1) + [(0, LANES - a.shape[-1])])


def _prepare(norm_ffn1, ffn1_w_gu, ffn1_w_down, norm_mix, ab_w_in, ssm_conv_w, ssm_conv_b,
             ssm_dt_bias, ssm_a_log, ssm_d, ssm_norm_w, gla_w_gk, gla_b_gk, gla_norm_w, ab_w_out,
             c_w_in, hgrn_lb_logits, hgrn_norm_w, c_w_out, norm_ffn2, ffn2_w_gu, ffn2_w_down,
             norm_final):
    depth, d = norm_ffn1.shape
    n_ab = ab_w_in.shape[0]
    n_c = c_w_in.shape[0]
    d_a = ssm_norm_w.shape[1]
    n_heads_a = ssm_d.shape[1]
    conv_dim = ssm_conv_w.shape[2]
    rank = gla_w_gk.shape[1]
    qk_w = gla_w_gk.shape[2]
    dv_b = gla_norm_w.shape[1]
    v_w = ab_w_out.shape[1] - d_a
    n_heads_b = v_w // dv_b
    dk_b = qk_w // n_heads_b
    dv_c = hgrn_norm_w.shape[1]
    d_c = hgrn_lb_logits.shape[1]
    n_heads_c = c_w_out.shape[1] // dv_c
    dk_c = d_c // n_heads_c
    assert 2 * rank <= LANES and n_heads_a + rank <= LANES

    ffn1 = [(norm_ffn1[l][None], ffn1_w_gu[l].astype(BF16), ffn1_w_down[l].astype(BF16))
            for l in range(depth)]
    ffn2 = [(norm_ffn2[l][None], ffn2_w_gu[l].astype(BF16), ffn2_w_down[l].astype(BF16))
            for l in range(depth)]
    rexp = (jnp.arange(LANES)[:, None] == (jnp.arange(d_a) // P_A)[None, :]).astype(F32)
    o_xbc, o_dt = d_a, d_a + conv_dim
    o_q = o_dt + n_heads_a
    o_glr = o_q + 2 * qk_w + 2 * v_w
    ab = []
    for j in range(n_ab):
        w_in = ab_w_in[j]
        w_main = jnp.concatenate([w_in[:, :o_dt], w_in[:, o_q:o_glr]], axis=1).astype(BF16)
        w_small = _pad_lanes(jnp.concatenate([w_in[:, o_dt:o_q], w_in[:, o_glr:]], axis=1))
        w_gk = jnp.zeros((LANES, qk_w), F32).at[n_heads_a:n_heads_a + rank].set(gla_w_gk[j])
        ab.append(dict(
            norm=norm_mix[2 * j][None], w_main=w_main, w_small=w_small.astype(BF16),
            conv_w=ssm_conv_w[j], conv_b=ssm_conv_b[j][None],
            dt_bias=_pad_lanes(ssm_dt_bias[j][None]), a_log=_pad_lanes(ssm_a_log[j][None]),
            dexp=jnp.repeat(ssm_d[j], P_A)[None], ssm_norm=ssm_norm_w[j][None],
            w_gk=w_gk.astype(BF16), b_gk=gla_b_gk[j][None], gla_norm=gla_norm_w[j][None],
            w_out=ab_w_out[j].astype(BF16), rexp=rexp, gla_shape=(n_heads_b, dk_b, dv_b)))
    cc = []
    for j in range(n_c):
        cc.append(dict(norm=norm_mix[2 * j + 1][None], w_in=c_w_in[j].astype(BF16),
                       lb_logits=hgrn_lb_logits, hgrn_norm=hgrn_norm_w[j][None],
                       w_out=c_w_out[j].astype(BF16), shape=(n_heads_c, dk_c, dv_c)))
    return dict(ffn1=ffn1, ffn2=ffn2, ab=ab, cc=cc, final=norm_final[None], depth=depth)


def _trunk(x, states, w):
    bsz, t, d = x.shape
    depth = w["depth"]
    h = x
    new_ssm, new_conv, new_gla, new_hgrn = [], [], [], []
    for l in range(depth):
        h = _ffn(h.reshape(bsz * t, d), *w["ffn1"][l]).reshape(bsz, t, d)
        j = l // 2
        if l % 2 == 0:
            st = None if states is None else (states[0][j], states[1][j], states[2][j])
            h, s_ssm, s_conv, s_gla = _mixer_ab(h, w["ab"][j], st)
            new_ssm.append(s_ssm)
            new_conv.append(s_conv)
            new_gla.append(s_gla)
        else:
            st = None if states is None else states[3][j]
            h, s_h = _mixer_c(h, w["cc"][j], j, st)
            new_hgrn.append(s_h)
        final_w = w["final"] if l == depth - 1 else None
        h = _ffn(h.reshape(bsz * t, d), *w["ffn2"][l], final_w=final_w).reshape(bsz, t, d)
    return h, jnp.stack(new_ssm), jnp.stack(new_conv), jnp.stack(new_gla), jnp.stack(new_hgrn)


def kernel(x_prompt, x_sample, state_ssm, state_conv, state_gla, state_hgrn, norm_ffn1, ffn1_w_gu, ffn1_w_down, norm_mix, ab_w_in, ssm_conv_w, ssm_conv_b, ssm_dt_bias, ssm_a_log, ssm_d, ssm_norm_w, gla_w_gk, gla_b_gk, gla_norm_w, ab_w_out, c_w_in, hgrn_lb_logits, hgrn_norm_w, c_w_out, norm_ffn2, ffn2_w_gu, ffn2_w_down, norm_final):
    w = _prepare(norm_ffn1, ffn1_w_gu, ffn1_w_down, norm_mix, ab_w_in, ssm_conv_w, ssm_conv_b,
                 ssm_dt_bias, ssm_a_log, ssm_d, ssm_norm_w, gla_w_gk, gla_b_gk, gla_norm_w,
                 ab_w_out, c_w_in, hgrn_lb_logits, hgrn_norm_w, c_w_out, norm_ffn2, ffn2_w_gu,
                 ffn2_w_down, norm_final)
    y_p, ssm_p, conv_p, gla_p, hgrn_p = _trunk(x_prompt, None, w)
    y_s, ssm_s, conv_s, gla_s, hgrn_s = _trunk(
        x_sample, (state_ssm, state_conv, state_gla, state_hgrn), w)
    return (y_p, y_s, ssm_p, conv_p, gla_p, hgrn_p, ssm_s, conv_s, gla_s, hgrn_s)
```

```python
import functools

import jax
import jax.numpy as jnp
from jax import lax
from jax.experimental import pallas as pl
from jax.experimental.pallas import tpu as pltpu

F32 = jnp.float32
BF16 = jnp.bfloat16

EPS = 1e-6
TINY = 1e-30
CONV_K = 4
P_A = 64
N_A = 128
G_A = 2
GLA_GATE_NORM = 16.0
LOG2_E = 1.4426950408889634
LANES = 128
SUBLANES = 8
HEAD_GROUP = 16
CUM_GROUPS = 3
DT_GROUPS = 2
HIST_ROW = SUBLANES - (CONV_K - 1)

SCAN_CHUNK = 128
MIXER_TILE = 256
FFN_TILE = 512
FFN_COLS = 512
VMEM_LIMIT = 56 * 1024 * 1024


def _mm(a, b):
    return jnp.dot(a, b, preferred_element_type=F32)


def _mm_nt(a, b):
    return lax.dot_general(a, b, (((1,), (1,)), ((), ())), preferred_element_type=F32)


def _mm_tn(a, b):
    return lax.dot_general(a, b, (((0,), (0,)), ((), ())), preferred_element_type=F32)


def _rms(x, w):
    return x * lax.rsqrt(jnp.mean(x * x, axis=-1, keepdims=True) + EPS) * w


def _silu(x):
    return x * jax.nn.sigmoid(x)


def _softplus(x):
    return jnp.maximum(x, 0.0) + jnp.log1p(jnp.exp(-jnp.abs(x)))


def _resident(shape):
    nd = len(shape)
    return pl.BlockSpec(shape, lambda *_: (0,) * nd, pipeline_mode=pl.Buffered(1))


def _ffn_kernel(*refs, d_ff, final_norm):
    if final_norm:
        x_ref, nw_ref, wgu_ref, wd_ref, fw_ref, o_ref, act_ref = refs
    else:
        x_ref, nw_ref, wgu_ref, wd_ref, o_ref, act_ref = refs
    x = x_ref[...]
    xn = _rms(x, nw_ref[...]).astype(BF16)
    for c0 in range(0, d_ff, FFN_COLS):
        cw = min(FFN_COLS, d_ff - c0)
        gate = _mm(xn, wgu_ref[:, c0:c0 + cw])
        up = _mm(xn, wgu_ref[:, d_ff + c0:d_ff + c0 + cw])
        act_ref[:, c0:c0 + cw] = (_silu(gate) * up).astype(BF16)
    y = x + 0.5 * _mm(act_ref[...], wd_ref[...])
    if final_norm:
        y = _rms(y, fw_ref[...])
    o_ref[...] = y


def _ffn(x2d, nw, wgu, wd, final_w=None):
    m, d = x2d.shape
    d_ff = wd.shape[0]
    tm = min(FFN_TILE, m)
    assert m % tm == 0
    final_norm = final_w is not None
    in_specs = [pl.BlockSpec((tm, d), lambda i: (i, 0)), _resident((1, d)),
                _resident(wgu.shape), _resident(wd.shape)]
    args = [x2d, nw, wgu, wd]
    if final_norm:
        in_specs.append(_resident((1, d)))
        args.append(final_w)
    return pl.pallas_call(
        functools.partial(_ffn_kernel, d_ff=d_ff, final_norm=final_norm),
        grid=(m // tm,),
        in_specs=in_specs,
        out_specs=pl.BlockSpec((tm, d), lambda i: (i, 0)),
        out_shape=jax.ShapeDtypeStruct((m, d), F32),
        scratch_shapes=[pltpu.VMEM((tm, d_ff), BF16)],
        compiler_params=pltpu.CompilerParams(dimension_semantics=("parallel",),
                                             vmem_limit_bytes=VMEM_LIMIT),
        name="ffn_final" if final_norm else "ffn",
    )(*args)


def _chunk_masks(length):
    i = lax.broadcasted_iota(jnp.int32, (length, length), 0)
    j = lax.broadcasted_iota(jnp.int32, (length, length), 1)
    return i >= j, i == j, jnp.where(i > j, i ^ j, 0)


def _merge_blocks(pre, suf, s):
    length, kdim = pre.shape
    tot = pre if suf is None else pre * suf
    if suf is None:
        suf = jnp.ones_like(pre)
    if s >= SUBLANES:
        pres, sufs = [], []
        for b0 in range(0, length, 2 * s):
            lo, up = slice(b0, b0 + s), slice(b0 + s, b0 + 2 * s)
            pres += [pre[lo], pre[up] * tot[lo]]
            sufs += [suf[lo] * tot[up], suf[up]]
        return jnp.concatenate(pres, axis=0), jnp.concatenate(sufs, axis=0)
    tiles = tot.reshape(length // SUBLANES, SUBLANES, kdim)
    below = pltpu.roll(tiles, s, 1).reshape(length, kdim)
    above = pltpu.roll(tiles, SUBLANES - s, 1).reshape(length, kdim)
    upper = (lax.broadcasted_iota(jnp.int32, (length, kdim), 0) & s) != 0
    return pre * jnp.where(upper, below, 1.0), suf * jnp.where(upper, 1.0, above)


def _gated_chunk(q, k, v, g, st_ref, head, eye, split):
    length = q.shape[0]
    att = jnp.where(eye, _mm_nt(q.astype(BF16), k.astype(BF16)), 0.0)
    pre, suf = g, None
    s = 1
    while s < length:
        ks = k if suf is None else k * suf
        att = jnp.where(split >= s, _mm_nt((q * pre).astype(BF16), ks.astype(BF16)), att)
        pre, suf = _merge_blocks(pre, suf, s)
        s *= 2
    st = st_ref[head]
    o = _mm(att.astype(BF16), v.astype(BF16)) + _mm_nt((q * pre).astype(BF16), st.astype(BF16))
    st_ref[head] = (st * pre[length - 1:length, :]
                    + _mm_tn(v.astype(BF16), (k * suf).astype(BF16)))
    return o


def _head_norm_gate(o, w, gate):
    return _rms(o, w) * _silu(gate)


def _row_valid(shape, tv):
    return lax.broadcasted_iota(jnp.int32, shape, 0) < tv


def _split_bf16(x, parts):
    out = []
    for _ in range(parts - 1):
        hi = x.astype(BF16).astype(F32)
        out.append(hi)
        x = x - hi
    out.append(x)
    return out


def _ssd_chunk(xs, bm, cm, dt, a2_row, rexp_cum, rexp_dt, st_ref, causal):
    length = xs.shape[0]
    heads = xs.shape[1] // P_A
    pair_w = 2 * P_A
    group_w = xs.shape[1] // G_A
    pairs_per_group = heads // 2 // G_A
    row = lax.broadcasted_iota(jnp.int32, (length, LANES), 0)
    grp = lax.broadcasted_iota(jnp.int32, (length, LANES), 1) // HEAD_GROUP
    cum = dt * a2_row
    sh = 1
    while sh < length:
        cum = cum + jnp.where(row >= sh, pltpu.roll(cum, sh, 0), 0.0)
        sh *= 2
    c_hi, c_mid, c_lo = _split_bf16(cum, 3)
    cum_cols = _mm(jnp.where(grp == 0, c_hi, jnp.where(grp == 1, c_mid, c_lo)).astype(BF16),
                   rexp_cum)
    d_hi, d_mid = _split_bf16(dt, 2)
    dt_x = _mm(jnp.where(grp == 3, d_hi, jnp.where(grp == 4, d_mid, 0.0)).astype(BF16), rexp_dt)
    cum_t = cum.T
    xdt = xs * dt_x
    lane = lax.broadcasted_iota(jnp.int32, (length, pair_w), 1)
    first = lane < P_A
    ys = []
    for g in range(G_A):
        bg = bm[:, g * N_A:(g + 1) * N_A].astype(BF16)
        cg = cm[:, g * N_A:(g + 1) * N_A].astype(BF16)
        cbs = jnp.where(causal, _mm_nt(cg, bg), 0.0)
        intra, from_start, to_end, end_decay = [], [], [], []
        for pr in range(pairs_per_group):
            hp = g * pairs_per_group + pr
            sl = slice(hp * pair_w, (hp + 1) * pair_w)
            ms, cols = [], []
            for hh in (2 * hp, 2 * hp + 1):
                col = cum_cols[:, hh * LANES:(hh + 1) * LANES]
                cols.append(col)
                ms.append((cbs * jnp.exp2(jnp.minimum(col - cum_t[hh:hh + 1, :], 0.0))).astype(BF16))
            xp = xdt[:, sl]
            xpb = xp.astype(BF16)
            zero = jnp.zeros_like(xpb)
            intra.append(_mm(jnp.concatenate(ms, axis=1),
                             jnp.concatenate([jnp.where(first, xpb, zero),
                                              jnp.where(first, zero, xpb)], axis=0)))
            cpair = jnp.where(first, cols[0], cols[1])
            last = cpair[length - 1:length, :]
            from_start.append(jnp.exp2(cpair))
            to_end.append((xp * jnp.exp2(last - cpair)).astype(BF16))
            end_decay.append(jnp.exp2(last))
        gs = slice(g * group_w, (g + 1) * group_w)
        st = st_ref[:, gs]
        ys.append(jnp.concatenate(intra, axis=1)
                  + _mm(cg, st.astype(BF16)) * jnp.concatenate(from_start, axis=1))
        st_ref[:, gs] = (st * jnp.concatenate(end_decay, axis=1)
                         + _mm_tn(bg, jnp.concatenate(to_end, axis=1)))
    return jnp.concatenate(ys, axis=1)


def _mixer_ab_kernel(*refs, tt_in, tt, tv, has_state):
    (h_ref, nw_ref, wm_ref, ws_ref, cw_ref, cb_ref, dtb_ref, alog_ref, dexp_ref, snw_ref,
     wgk_ref, bgk_ref, gnw_ref, wo_ref, rcum_ref, rdt_ref) = refs[:16]
    refs = refs[16:]
    if has_state:
        ssm0_ref, conv0_ref, gla0_ref = refs[:3]
        refs = refs[3:]
    ho_ref, ssm_o_ref, conv_o_ref, gla_o_ref, ext_ref, ssm_ref, gla_ref = refs

    d_a = dexp_ref.shape[1]
    conv_dim = cw_ref.shape[1]
    n_heads_b, dv_b, dk_b = gla_ref.shape
    qk_w = n_heads_b * dk_b
    v_w = n_heads_b * dv_b
    n_pairs = d_a // (2 * P_A)
    c_xbc, c_q, c_k, c_v, c_g = d_a, d_a + conv_dim, d_a + conv_dim + qk_w, \
        d_a + conv_dim + 2 * qk_w, d_a + conv_dim + 2 * qk_w + v_w
    t = pl.program_id(1)
    last_t = pl.num_programs(1) - 1

    @pl.when(t == 0)
    def _():
        if has_state:
            for hp in range(n_pairs):
                sl = slice(hp * 2 * P_A, (hp + 1) * 2 * P_A)
                ssm_ref[:, sl] = ssm0_ref[sl, :].T
            ext_ref[HIST_ROW:SUBLANES, :] = conv0_ref[...]
            for hd in range(n_heads_b):
                gla_ref[hd] = gla0_ref[hd].T
        else:
            ssm_ref[...] = jnp.zeros_like(ssm_ref)
            ext_ref[HIST_ROW:SUBLANES, :] = jnp.zeros((CONV_K - 1, conv_dim), F32)
            gla_ref[...] = jnp.zeros_like(gla_ref)

    h = h_ref[...]
    xn = _rms(h, nw_ref[...]).astype(BF16)
    if tt_in < tt:
        xn = jnp.concatenate([xn, jnp.zeros((tt - tt_in, xn.shape[1]), BF16)], axis=0)
    causal, eye, split = _chunk_masks(SCAN_CHUNK)

    ext_ref[SUBLANES:SUBLANES + tt, :] = _mm(xn, wm_ref[:, c_xbc:c_q])
    conv = cb_ref[...]
    for kk in range(CONV_K):
        conv = conv + cw_ref[kk:kk + 1, :] * ext_ref[HIST_ROW + kk:HIST_ROW + kk + tt, :]
    hist = ext_ref[HIST_ROW + tv:SUBLANES + tv, :]
    ext_ref[HIST_ROW:SUBLANES, :] = hist

    @pl.when(t == last_t)
    def _():
        conv_o_ref[...] = hist

    xbc = _silu(conv)
    xs = xbc[:, :d_a]
    bm = xbc[:, d_a:d_a + G_A * N_A]
    cm = xbc[:, d_a + G_A * N_A:]
    small = _mm(xn, ws_ref[...])
    dt = _softplus(small + dtb_ref[...])
    if tv < tt:
        dt = jnp.where(_row_valid(dt.shape, tv), dt, 0.0)
    lane = lax.broadcasted_iota(jnp.int32, (1, LANES), 1)
    a2_row = jnp.where(lane < CUM_GROUPS * HEAD_GROUP, -LOG2_E * jnp.exp(alog_ref[...]), 0.0)
    ys = []
    for c0 in range(0, tt, SCAN_CHUNK):
        rs = slice(c0, c0 + SCAN_CHUNK)
        ys.append(_ssd_chunk(xs[rs], bm[rs], cm[rs], dt[rs], a2_row, rcum_ref[...], rdt_ref[...],
                             ssm_ref, causal))
    y = jnp.concatenate(ys, axis=0) if len(ys) > 1 else ys[0]
    y = (y + dexp_ref[...] * xs) * _silu(_mm(xn, wm_ref[:, 0:c_xbc]))
    gw = d_a // G_A
    y = jnp.concatenate([_rms(y[:, g * gw:(g + 1) * gw], snw_ref[:, g * gw:(g + 1) * gw])
                         for g in range(G_A)], axis=1)
    acc = _mm(y.astype(BF16), wo_ref[0:d_a, :])

    q = _mm(xn, wm_ref[:, c_q:c_k]) * dk_b ** -0.5
    k = _mm(xn, wm_ref[:, c_k:c_v])
    v = _mm(xn, wm_ref[:, c_v:c_g])
    gate = _mm(xn, wm_ref[:, c_g:c_g + v_w])
    gk = _mm(small.astype(BF16), wgk_ref[...]) + bgk_ref[...]
    decay = jnp.exp(-_softplus(-gk) / GLA_GATE_NORM)
    if tv < tt:
        decay = jnp.where(_row_valid(decay.shape, tv), decay, 1.0)
        k = jnp.where(_row_valid(k.shape, tv), k, 0.0)
    rows = []
    for c0 in range(0, tt, SCAN_CHUNK):
        rs = slice(c0, c0 + SCAN_CHUNK)
        heads = []
        for hd in range(n_heads_b):
            ks = slice(hd * dk_b, (hd + 1) * dk_b)
            vs = slice(hd * dv_b, (hd + 1) * dv_b)
            o = _gated_chunk(q[rs, ks], k[rs, ks], v[rs, vs], decay[rs, ks], gla_ref, hd, eye,
                             split)
            heads.append(_head_norm_gate(o, gnw_ref[...], gate[rs, vs]))
        rows.append(jnp.concatenate(heads, axis=1))
    o = jnp.concatenate(rows, axis=0) if len(rows) > 1 else rows[0]
    acc = acc + _mm(o.astype(BF16), wo_ref[d_a:d_a + v_w, :])
    ho_ref[...] = h + acc[:tt_in]

    @pl.when(t == last_t)
    def _():
        for hp in range(n_pairs):
            sl = slice(hp * 2 * P_A, (hp + 1) * 2 * P_A)
            ssm_o_ref[sl, :] = ssm_ref[:, sl].T
        for hd in range(n_heads_b):
            gla_o_ref[hd] = gla_ref[hd].T


def _tiles(t):
    if t % MIXER_TILE == 0:
        return MIXER_TILE, MIXER_TILE, MIXER_TILE
    assert t <= SCAN_CHUNK and t % (2 * SUBLANES) == 0
    return t, SCAN_CHUNK, t


def _batch_block(shape):
    nd = len(shape)
    return pl.BlockSpec((None,) + tuple(shape[1:]), lambda b, t: (b,) + (0,) * (nd - 1))


def _mixer_ab(h, w, states):
    bsz, t, d = h.shape
    tt_in, tt, tv = _tiles(t)
    has_state = states is not None
    d_a = w["dexp"].shape[1]
    conv_dim = w["conv_w"].shape[1]
    n_heads_b, dk_b, dv_b = w["gla_shape"]
    consts = [w["norm"], w["w_main"], w["w_small"], w["conv_w"], w["conv_b"], w["dt_bias"],
              w["a_log"], w["dexp"], w["ssm_norm"], w["w_gk"], w["b_gk"], w["gla_norm"],
              w["w_out"], w["rexp_cum"], w["rexp_dt"]]
    args = [h] + consts
    in_specs = [pl.BlockSpec((None, tt_in, d), lambda b, i: (b, i, 0))]
    in_specs += [_resident(c.shape) for c in consts]
    if has_state:
        ssm0, conv0, gla0 = states
        ssm0 = ssm0.reshape(bsz, d_a, N_A)
        args += [ssm0, conv0, gla0]
        in_specs += [_batch_block(ssm0.shape), _batch_block(conv0.shape), _batch_block(gla0.shape)]
    out_shape = [jax.ShapeDtypeStruct((bsz, t, d), F32),
                 jax.ShapeDtypeStruct((bsz, d_a, N_A), F32),
                 jax.ShapeDtypeStruct((bsz, CONV_K - 1, conv_dim), F32),
                 jax.ShapeDtypeStruct((bsz, n_heads_b, dk_b, dv_b), F32)]
    out_specs = [pl.BlockSpec((None, tt_in, d), lambda b, i: (b, i, 0))]
    out_specs += [_batch_block(s.shape) for s in out_shape[1:]]
    ho, ssm, conv, gla = pl.pallas_call(
        functools.partial(_mixer_ab_kernel, tt_in=tt_in, tt=tt, tv=tv, has_state=has_state),
        grid=(bsz, t // tt_in),
        in_specs=in_specs,
        out_specs=out_specs,
        out_shape=out_shape,
        scratch_shapes=[pltpu.VMEM((SUBLANES + tt, conv_dim), F32),
                        pltpu.VMEM((N_A, d_a), F32),
                        pltpu.VMEM((n_heads_b, dv_b, dk_b), F32)],
        compiler_params=pltpu.CompilerParams(dimension_semantics=("parallel", "arbitrary"),
                                             vmem_limit_bytes=VMEM_LIMIT),
        name="mixer_ab_state" if has_state else "mixer_ab",
    )(*args)
    return ho, ssm.reshape(bsz, d_a // P_A, P_A, N_A), conv, gla


def _mixer_c_kernel(*refs, tt_in, tt, tv, layer, has_state):
    h_ref, nw_ref, wi_ref, lbl_ref, hnw_ref, wo_ref = refs[:6]
    refs = refs[6:]
    if has_state:
        st0_ref = refs[0]
        refs = refs[1:]
    ho_ref, st_o_ref, st_ref = refs
    n_heads, dv, dk = st_ref.shape
    d_c = n_heads * dk
    v_w = n_heads * dv
    t = pl.program_id(1)
    last_t = pl.num_programs(1) - 1

    @pl.when(t == 0)
    def _():
        if has_state:
            for hd in range(n_heads):
                st_ref[hd] = st0_ref[hd].T
        else:
            st_ref[...] = jnp.zeros_like(st_ref)

    logits = lbl_ref[...]
    e = jnp.exp(logits - jnp.max(logits, axis=0, keepdims=True))
    sm = e / jnp.sum(e, axis=0, keepdims=True)
    lb = jnp.sum(sm[0:layer + 1, :], axis=0, keepdims=True) - sm[0:1, :]

    h = h_ref[...]
    xn = _rms(h, nw_ref[...]).astype(BF16)
    if tt_in < tt:
        xn = jnp.concatenate([xn, jnp.zeros((tt - tt_in, xn.shape[1]), BF16)], axis=0)
    _, eye, split = _chunk_masks(SCAN_CHUNK)

    q = _silu(_mm(xn, wi_ref[:, 0:d_c]))
    forget = lb + (1.0 - lb) * jax.nn.sigmoid(_mm(xn, wi_ref[:, d_c:2 * d_c]))
    decay = jnp.maximum(forget, TINY)
    k = 1.0 - forget
    if tv < tt:
        decay = jnp.where(_row_valid(decay.shape, tv), decay, 1.0)
        k = jnp.where(_row_valid(k.shape, tv), k, 0.0)
    v = _mm(xn, wi_ref[:, 2 * d_c:2 * d_c + v_w])
    gate = _mm(xn, wi_ref[:, 2 * d_c + v_w:2 * d_c + 2 * v_w])
    rows = []
    for c0 in range(0, tt, SCAN_CHUNK):
        rs = slice(c0, c0 + SCAN_CHUNK)
        heads = []
        for hd in range(n_heads):
            ks = slice(hd * dk, (hd + 1) * dk)
            vs = slice(hd * dv, (hd + 1) * dv)
            o = _gated_chunk(q[rs, ks], k[rs, ks], v[rs, vs], decay[rs, ks], st_ref, hd, eye, split)
            heads.append(_head_norm_gate(o, hnw_ref[...], gate[rs, vs]))
        rows.append(jnp.concatenate(heads, axis=1))
    o = jnp.concatenate(rows, axis=0) if len(rows) > 1 else rows[0]
    ho_ref[...] = h + _mm(o.astype(BF16), wo_ref[...])[:tt_in]

    @pl.when(t == last_t)
    def _():
        for hd in range(n_heads):
            st_o_ref[hd] = st_ref[hd].T


def _mixer_c(h, w, layer, state):
    bsz, t, d = h.shape
    tt_in, tt, tv = _tiles(t)
    has_state = state is not None
    n_heads, dk, dv = w["shape"]
    consts = [w["norm"], w["w_in"], w["lb_logits"], w["hgrn_norm"], w["w_out"]]
    args = [h] + consts
    in_specs = [pl.BlockSpec((None, tt_in, d), lambda b, i: (b, i, 0))]
    in_specs += [_resident(c.shape) for c in consts]
    if has_state:
        args.append(state)
        in_specs.append(_batch_block(state.shape))
    out_shape = [jax.ShapeDtypeStruct((bsz, t, d), F32),
                 jax.ShapeDtypeStruct((bsz, n_heads, dk, dv), F32)]
    out_specs = [pl.BlockSpec((None, tt_in, d), lambda b, i: (b, i, 0)),
                 _batch_block(out_shape[1].shape)]
    return pl.pallas_call(
        functools.partial(_mixer_c_kernel, tt_in=tt_in, tt=tt, tv=tv, layer=layer,
                          has_state=has_state),
        grid=(bsz, t // tt_in),
        in_specs=in_specs,
        out_specs=out_specs,
        out_shape=out_shape,
        scratch_shapes=[pltpu.VMEM((n_heads, dv, dk), F32)],
        compiler_params=pltpu.CompilerParams(dimension_semantics=("parallel", "arbitrary"),
                                             vmem_limit_bytes=VMEM_LIMIT),
        name="mixer_c_state" if has_state else "mixer_c",
    )(*args)


def _pad_lanes(a):
    return jnp.pad(a, [(0, 0)] * (a.ndim - 1) + [(0, LANES - a.shape[-1])])


def _prepare(norm_ffn1, ffn1_w_gu, ffn1_w_down, norm_mix, ab_w_in, ssm_conv_w, ssm_conv_b,
             ssm_dt_bias, ssm_a_log, ssm_d, ssm_norm_w, gla_w_gk, gla_b_gk, gla_norm_w, ab_w_out,
             c_w_in, hgrn_lb_logits, hgrn_norm_w, c_w_out, norm_ffn2, ffn2_w_gu, ffn2_w_down,
             norm_final):
    depth, d = norm_ffn1.shape
    n_ab = ab_w_in.shape[0]
    n_c = c_w_in.shape[0]
    d_a = ssm_norm_w.shape[1]
    n_heads_a = ssm_d.shape[1]
    conv_dim = ssm_conv_w.shape[2]
    rank = gla_w_gk.shape[1]
    qk_w = gla_w_gk.shape[2]
    dv_b = gla_norm_w.shape[1]
    v_w = ab_w_out.shape[1] - d_a
    n_heads_b = v_w // dv_b
    dk_b = qk_w // n_heads_b
    dv_c = hgrn_norm_w.shape[1]
    d_c = hgrn_lb_logits.shape[1]
    n_heads_c = c_w_out.shape[1] // dv_c
    dk_c = d_c // n_heads_c
    copies = CUM_GROUPS + DT_GROUPS
    assert n_heads_a == HEAD_GROUP and copies * HEAD_GROUP + rank <= LANES

    ffn1 = [(norm_ffn1[l][None], ffn1_w_gu[l].astype(BF16), ffn1_w_down[l].astype(BF16))
            for l in range(depth)]
    ffn2 = [(norm_ffn2[l][None], ffn2_w_gu[l].astype(BF16), ffn2_w_down[l].astype(BF16))
            for l in range(depth)]
    lane_head = jnp.arange(LANES) % HEAD_GROUP
    lane_grp = jnp.arange(LANES) // HEAD_GROUP
    rexp_cum = ((lane_head[:, None] == (jnp.arange(n_heads_a * LANES) // LANES)[None, :])
                & (lane_grp < CUM_GROUPS)[:, None]).astype(BF16)
    rexp_dt = ((lane_head[:, None] == (jnp.arange(d_a) // P_A)[None, :])
               & ((lane_grp >= CUM_GROUPS) & (lane_grp < copies))[:, None]).astype(BF16)
    glr0 = LANES - rank
    o_xbc, o_dt = d_a, d_a + conv_dim
    o_q = o_dt + n_heads_a
    o_glr = o_q + 2 * qk_w + 2 * v_w
    ab = []
    for j in range(n_ab):
        w_in = ab_w_in[j]
        w_main = jnp.concatenate([w_in[:, :o_dt], w_in[:, o_q:o_glr]], axis=1).astype(BF16)
        w_small = jnp.concatenate(
            [jnp.tile(w_in[:, o_dt:o_q], (1, copies)),
             jnp.zeros((d, glr0 - copies * HEAD_GROUP), F32), w_in[:, o_glr:]], axis=1)
        w_gk = jnp.zeros((LANES, qk_w), F32).at[glr0:].set(gla_w_gk[j])
        ab.append(dict(
            norm=norm_mix[2 * j][None], w_main=w_main, w_small=w_small.astype(BF16),
            conv_w=ssm_conv_w[j], conv_b=ssm_conv_b[j][None],
            dt_bias=_pad_lanes(jnp.tile(ssm_dt_bias[j], copies)[None]),
            a_log=_pad_lanes(jnp.tile(ssm_a_log[j], CUM_GROUPS)[None]),
            dexp=jnp.repeat(ssm_d[j], P_A)[None], ssm_norm=ssm_norm_w[j][None],
            w_gk=w_gk.astype(BF16), b_gk=gla_b_gk[j][None], gla_norm=gla_norm_w[j][None],
            w_out=ab_w_out[j].astype(BF16), rexp_cum=rexp_cum, rexp_dt=rexp_dt,
            gla_shape=(n_heads_b, dk_b, dv_b)))
    cc = []
    for j in range(n_c):
        cc.append(dict(norm=norm_mix[2 * j + 1][None], w_in=c_w_in[j].astype(BF16),
                       lb_logits=hgrn_lb_logits, hgrn_norm=hgrn_norm_w[j][None],
                       w_out=c_w_out[j].astype(BF16), shape=(n_heads_c, dk_c, dv_c)))
    return dict(ffn1=ffn1, ffn2=ffn2, ab=ab, cc=cc, final=norm_final[None], depth=depth)


def _trunk(x, states, w):
    bsz, t, d = x.shape
    depth = w["depth"]
    h = x
    new_ssm, new_conv, new_gla, new_hgrn = [], [], [], []
    for l in range(depth):
        h = _ffn(h.reshape(bsz * t, d), *w["ffn1"][l]).reshape(bsz, t, d)
        j = l // 2
        if l % 2 == 0:
            st = None if states is None else (states[0][j], states[1][j], states[2][j])
            h, s_ssm, s_conv, s_gla = _mixer_ab(h, w["ab"][j], st)
            new_ssm.append(s_ssm)
            new_conv.append(s_conv)
            new_gla.append(s_gla)
        else:
            st = None if states is None else states[3][j]
            h, s_h = _mixer_c(h, w["cc"][j], j, st)
            new_hgrn.append(s_h)
        final_w = w["final"] if l == depth - 1 else None
        h = _ffn(h.reshape(bsz * t, d), *w["ffn2"][l], final_w=final_w).reshape(bsz, t, d)
    return h, jnp.stack(new_ssm), jnp.stack(new_conv), jnp.stack(new_gla), jnp.stack(new_hgrn)


def kernel(x_prompt, x_sample, state_ssm, state_conv, state_gla, state_hgrn, norm_ffn1, ffn1_w_gu, ffn1_w_down, norm_mix, ab_w_in, ssm_conv_w, ssm_conv_b, ssm_dt_bias, ssm_a_log, ssm_d, ssm_norm_w, gla_w_gk, gla_b_gk, gla_norm_w, ab_w_out, c_w_in, hgrn_lb_logits, hgrn_norm_w, c_w_out, norm_ffn2, ffn2_w_gu, ffn2_w_down, norm_final):
    w = _prepare(norm_ffn1, ffn1_w_gu, ffn1_w_down, norm_mix, ab_w_in, ssm_conv_w, ssm_conv_b,
                 ssm_dt_bias, ssm_a_log, ssm_d, ssm_norm_w, gla_w_gk, gla_b_gk, gla_norm_w,
                 ab_w_out, c_w_in, hgrn_lb_logits, hgrn_norm_w, c_w_out, norm_ffn2, ffn2_w_gu,
                 ffn2_w_down, norm_final)
    y_p, ssm_p, conv_p, gla_p, hgrn_p = _trunk(x_prompt, None, w)
    y_s, ssm_s, conv_s, gla_s, hgrn_s = _trunk(
        x_sample, (state_ssm, state_conv, state_gla, state_hgrn), w)
    return (y_p, y_s, ssm_p, conv_p, gla_p, hgrn_p, ssm_s, conv_s, gla_s, hgrn_s)
```

```python
import functools

import jax
import jax.numpy as jnp
from jax import lax
from jax.experimental import pallas as pl
from jax.experimental.pallas import tpu as pltpu

F32 = jnp.float32
BF16 = jnp.bfloat16

EPS = 1e-6
TINY = 1e-30
CONV_K = 4
P_A = 64
N_A = 128
G_A = 2
GLA_GATE_NORM = 16.0
LOG2_E = 1.4426950408889634
LANES = 128
SUBLANES = 8
HEAD_GROUP = 16
CUM_GROUPS = 3
DT_GROUPS = 2
HIST_ROW = SUBLANES - (CONV_K - 1)

MXU_COLS = 256
SCAN_CHUNK = 128
MIXER_TILE = 256
FFN_TILE = 512
FFN_COLS = 512
VMEM_LIMIT = 56 * 1024 * 1024


def _mm(a, b):
    return jnp.dot(a, b, preferred_element_type=F32)


def _mm_nt(a, b):
    return lax.dot_general(a, b, (((1,), (1,)), ((), ())), preferred_element_type=F32)


def _mm_tn(a, b):
    return lax.dot_general(a, b, (((0,), (0,)), ((), ())), preferred_element_type=F32)


def _rms(x, w):
    return x * lax.rsqrt(jnp.mean(x * x, axis=-1, keepdims=True) + EPS) * w


def _silu(x):
    return x * jax.nn.sigmoid(x)


def _softplus(x):
    return jnp.maximum(x, 0.0) + jnp.log1p(jnp.exp(-jnp.abs(x)))


def _resident(shape):
    nd = len(shape)
    return pl.BlockSpec(shape, lambda *_: (0,) * nd, pipeline_mode=pl.Buffered(1))


def _ffn_kernel(*refs, d_ff, final_norm):
    if final_norm:
        x_ref, nw_ref, wgu_ref, wd_ref, fw_ref, o_ref, act_ref = refs
    else:
        x_ref, nw_ref, wgu_ref, wd_ref, o_ref, act_ref = refs
    x = x_ref[...]
    xn = _rms(x, nw_ref[...]).astype(BF16)
    for c0 in range(0, d_ff, FFN_COLS):
        cw = min(FFN_COLS, d_ff - c0)
        gate = _mm(xn, wgu_ref[:, c0:c0 + cw])
        up = _mm(xn, wgu_ref[:, d_ff + c0:d_ff + c0 + cw])
        act_ref[:, c0:c0 + cw] = (_silu(gate) * up).astype(BF16)
    y = x + 0.5 * _mm(act_ref[...], wd_ref[...])
    if final_norm:
        y = _rms(y, fw_ref[...])
    o_ref[...] = y


def _ffn(x2d, nw, wgu, wd, final_w=None):
    m, d = x2d.shape
    d_ff = wd.shape[0]
    tm = min(FFN_TILE, m)
    assert m % tm == 0
    final_norm = final_w is not None
    in_specs = [pl.BlockSpec((tm, d), lambda i: (i, 0)), _resident((1, d)),
                _resident(wgu.shape), _resident(wd.shape)]
    args = [x2d, nw, wgu, wd]
    if final_norm:
        in_specs.append(_resident((1, d)))
        args.append(final_w)
    return pl.pallas_call(
        functools.partial(_ffn_kernel, d_ff=d_ff, final_norm=final_norm),
        grid=(m // tm,),
        in_specs=in_specs,
        out_specs=pl.BlockSpec((tm, d), lambda i: (i, 0)),
        out_shape=jax.ShapeDtypeStruct((m, d), F32),
        scratch_shapes=[pltpu.VMEM((tm, d_ff), BF16)],
        compiler_params=pltpu.CompilerParams(dimension_semantics=("parallel",),
                                             vmem_limit_bytes=VMEM_LIMIT),
        name="ffn_final" if final_norm else "ffn",
    )(*args)


def _chunk_masks(length):
    i = lax.broadcasted_iota(jnp.int32, (length, length), 0)
    j = lax.broadcasted_iota(jnp.int32, (length, length), 1)
    return i >= j, i == j, jnp.where(i > j, i ^ j, 0)


def _merge_blocks(pre, suf, s):
    length, kdim = pre.shape
    tot = pre if suf is None else pre * suf
    if suf is None:
        suf = jnp.ones_like(pre)
    if s >= SUBLANES:
        pres, sufs = [], []
        for b0 in range(0, length, 2 * s):
            lo, up = slice(b0, b0 + s), slice(b0 + s, b0 + 2 * s)
            pres += [pre[lo], pre[up] * tot[lo]]
            sufs += [suf[lo] * tot[up], suf[up]]
        return jnp.concatenate(pres, axis=0), jnp.concatenate(sufs, axis=0)
    tiles = tot.reshape(length // SUBLANES, SUBLANES, kdim)
    below = pltpu.roll(tiles, s, 1).reshape(length, kdim)
    above = pltpu.roll(tiles, SUBLANES - s, 1).reshape(length, kdim)
    upper = (lax.broadcasted_iota(jnp.int32, (length, kdim), 0) & s) != 0
    return pre * jnp.where(upper, below, 1.0), suf * jnp.where(upper, 1.0, above)


def _col_pieces(x, w_ref, row0, row1, c0, c1):
    return [functools.partial(lambda a, b: _mm(x, w_ref[row0:row1, a:b]), c, min(c + MXU_COLS, c1))
            for c in range(c0, c1, MXU_COLS)]


def _run_one(side, done):
    if side:
        done.append(side.pop(0)())


def _run_all(side, done):
    while side:
        _run_one(side, done)


def _gated_chunks(items, st_ref, eye, split, side, done):
    length = items[0][1].shape[0]
    att = [jnp.where(eye, _mm_nt(q.astype(BF16), k.astype(BF16)), 0.0) for _, q, k, _, _ in items]
    pre = [g for _, _, _, _, g in items]
    suf = [None] * len(items)
    s = 1
    while s < length:
        _run_one(side, done)
        for i, (_, q, k, _, _) in enumerate(items):
            ks = k if suf[i] is None else k * suf[i]
            att[i] = jnp.where(split >= s, _mm_nt((q * pre[i]).astype(BF16), ks.astype(BF16)),
                               att[i])
            pre[i], suf[i] = _merge_blocks(pre[i], suf[i], s)
        s *= 2
    outs = []
    for i, (head, q, k, v, _) in enumerate(items):
        st = st_ref[head]
        outs.append(_mm(att[i].astype(BF16), v.astype(BF16))
                    + _mm_nt((q * pre[i]).astype(BF16), st.astype(BF16)))
        st_ref[head] = (st * pre[i][length - 1:length, :]
                        + _mm_tn(v.astype(BF16), (k * suf[i]).astype(BF16)))
    return outs


def _head_norm_gate(o, w, gate):
    return _rms(o, w) * _silu(gate)


def _row_valid(shape, tv):
    return lax.broadcasted_iota(jnp.int32, shape, 0) < tv


def _split_bf16(x, parts):
    out = []
    for _ in range(parts - 1):
        hi = x.astype(BF16).astype(F32)
        out.append(hi)
        x = x - hi
    out.append(x)
    return out


def _ssd_chunk(xs, bm, cm, dt, a2_row, rexp_cum, rexp_dt, st_ref, causal, side, done):
    length = xs.shape[0]
    heads = xs.shape[1] // P_A
    pair_w = 2 * P_A
    group_w = xs.shape[1] // G_A
    pairs_per_group = heads // 2 // G_A
    row = lax.broadcasted_iota(jnp.int32, (length, LANES), 0)
    grp = lax.broadcasted_iota(jnp.int32, (length, LANES), 1) // HEAD_GROUP
    cum = dt * a2_row
    sh = 1
    while sh < length:
        cum = cum + jnp.where(row >= sh, pltpu.roll(cum, sh, 0), 0.0)
        sh *= 2
    c_hi, c_mid, c_lo = _split_bf16(cum, 3)
    cum_cols = _mm(jnp.where(grp == 0, c_hi, jnp.where(grp == 1, c_mid, c_lo)).astype(BF16),
                   rexp_cum)
    d_hi, d_mid = _split_bf16(dt, 2)
    dt_x = _mm(jnp.where(grp == 3, d_hi, jnp.where(grp == 4, d_mid, 0.0)).astype(BF16), rexp_dt)
    cum_t = cum.T
    xdt = xs * dt_x
    lane = lax.broadcasted_iota(jnp.int32, (length, pair_w), 1)
    first = lane < P_A
    ys = []
    for g in range(G_A):
        bg = bm[:, g * N_A:(g + 1) * N_A].astype(BF16)
        cg = cm[:, g * N_A:(g + 1) * N_A].astype(BF16)
        cbs = jnp.where(causal, _mm_nt(cg, bg), 0.0)
        intra, from_start, to_end, end_decay = [], [], [], []
        for pr in range(pairs_per_group):
            _run_one(side, done)
            hp = g * pairs_per_group + pr
            sl = slice(hp * pair_w, (hp + 1) * pair_w)
            ms, cols = [], []
            for hh in (2 * hp, 2 * hp + 1):
                col = cum_cols[:, hh * LANES:(hh + 1) * LANES]
                cols.append(col)
                ms.append((cbs * jnp.exp2(jnp.minimum(col - cum_t[hh:hh + 1, :], 0.0))).astype(BF16))
            xp = xdt[:, sl]
            xpb = xp.astype(BF16)
            zero = jnp.zeros_like(xpb)
            intra.append(_mm(jnp.concatenate(ms, axis=1),
                             jnp.concatenate([jnp.where(first, xpb, zero),
                                              jnp.where(first, zero, xpb)], axis=0)))
            cpair = jnp.where(first, cols[0], cols[1])
            last = cpair[length - 1:length, :]
            from_start.append(jnp.exp2(cpair))
            to_end.append((xp * jnp.exp2(last - cpair)).astype(BF16))
            end_decay.append(jnp.exp2(last))
        gs = slice(g * group_w, (g + 1) * group_w)
        st = st_ref[:, gs]
        ys.append(jnp.concatenate(intra, axis=1)
                  + _mm(cg, st.astype(BF16)) * jnp.concatenate(from_start, axis=1))
        st_ref[:, gs] = (st * jnp.concatenate(end_decay, axis=1)
                         + _mm_tn(bg, jnp.concatenate(to_end, axis=1)))
    return jnp.concatenate(ys, axis=1)


def _mixer_ab_kernel(*refs, tt_in, tt, tv, has_state):
    (h_ref, nw_ref, wm_ref, ws_ref, cw_ref, cb_ref, dtb_ref, alog_ref, dexp_ref, snw_ref,
     wgk_ref, bgk_ref, gnw_ref, wo_ref, rcum_ref, rdt_ref) = refs[:16]
    refs = refs[16:]
    if has_state:
        ssm0_ref, conv0_ref, gla0_ref = refs[:3]
        refs = refs[3:]
    ho_ref, ssm_o_ref, conv_o_ref, gla_o_ref, ext_ref, ssm_ref, gla_ref = refs

    d_a = dexp_ref.shape[1]
    conv_dim = cw_ref.shape[1]
    n_heads_b, dv_b, dk_b = gla_ref.shape
    qk_w = n_heads_b * dk_b
    v_w = n_heads_b * dv_b
    n_pairs = d_a // (2 * P_A)
    c_xbc, c_q, c_k, c_v, c_g = d_a, d_a + conv_dim, d_a + conv_dim + qk_w, \
        d_a + conv_dim + 2 * qk_w, d_a + conv_dim + 2 * qk_w + v_w
    t = pl.program_id(1)
    last_t = pl.num_programs(1) - 1

    @pl.when(t == 0)
    def _():
        if has_state:
            for hp in range(n_pairs):
                sl = slice(hp * 2 * P_A, (hp + 1) * 2 * P_A)
                ssm_ref[:, sl] = ssm0_ref[sl, :].T
            ext_ref[HIST_ROW:SUBLANES, :] = conv0_ref[...]
            for hd in range(n_heads_b):
                gla_ref[hd] = gla0_ref[hd].T
        else:
            ssm_ref[...] = jnp.zeros_like(ssm_ref)
            ext_ref[HIST_ROW:SUBLANES, :] = jnp.zeros((CONV_K - 1, conv_dim), F32)
            gla_ref[...] = jnp.zeros_like(gla_ref)

    h = h_ref[...]
    xn = _rms(h, nw_ref[...]).astype(BF16)
    if tt_in < tt:
        xn = jnp.concatenate([xn, jnp.zeros((tt - tt_in, xn.shape[1]), BF16)], axis=0)
    causal, eye, split = _chunk_masks(SCAN_CHUNK)

    ext_ref[SUBLANES:SUBLANES + tt, :] = _mm(xn, wm_ref[:, c_xbc:c_q])
    conv = cb_ref[...]
    for kk in range(CONV_K):
        conv = conv + cw_ref[kk:kk + 1, :] * ext_ref[HIST_ROW + kk:HIST_ROW + kk + tt, :]
    hist = ext_ref[HIST_ROW + tv:SUBLANES + tv, :]
    ext_ref[HIST_ROW:SUBLANES, :] = hist

    @pl.when(t == last_t)
    def _():
        conv_o_ref[...] = hist

    xbc = _silu(conv)
    xs = xbc[:, :d_a]
    bm = xbc[:, d_a:d_a + G_A * N_A]
    cm = xbc[:, d_a + G_A * N_A:]
    small = _mm(xn, ws_ref[...])
    dt = _softplus(small + dtb_ref[...])
    if tv < tt:
        dt = jnp.where(_row_valid(dt.shape, tv), dt, 0.0)
    lane = lax.broadcasted_iota(jnp.int32, (1, LANES), 1)
    a2_row = jnp.where(lane < CUM_GROUPS * HEAD_GROUP, -LOG2_E * jnp.exp(alog_ref[...]), 0.0)
    rows_in = wm_ref.shape[0]
    side = (_col_pieces(xn, wm_ref, 0, rows_in, c_q, c_g)
            + _col_pieces(xn, wm_ref, 0, rows_in, 0, c_xbc))
    done = []
    ys = []
    for c0 in range(0, tt, SCAN_CHUNK):
        rs = slice(c0, c0 + SCAN_CHUNK)
        ys.append(_ssd_chunk(xs[rs], bm[rs], cm[rs], dt[rs], a2_row, rcum_ref[...], rdt_ref[...],
                             ssm_ref, causal, side, done))
    _run_all(side, done)
    qkv = jnp.concatenate(done[:(c_g - c_q) // MXU_COLS], axis=1)
    z = jnp.concatenate(done[(c_g - c_q) // MXU_COLS:], axis=1)
    y = jnp.concatenate(ys, axis=0) if len(ys) > 1 else ys[0]
    y = (y + dexp_ref[...] * xs) * _silu(z)
    gw = d_a // G_A
    y = jnp.concatenate([_rms(y[:, g * gw:(g + 1) * gw], snw_ref[:, g * gw:(g + 1) * gw])
                         for g in range(G_A)], axis=1).astype(BF16)

    q = qkv[:, :qk_w] * dk_b ** -0.5
    k = qkv[:, qk_w:2 * qk_w]
    v = qkv[:, 2 * qk_w:]
    gk = _mm(small.astype(BF16), wgk_ref[...]) + bgk_ref[...]
    decay = jnp.exp(-_softplus(-gk) / GLA_GATE_NORM)
    if tv < tt:
        decay = jnp.where(_row_valid(decay.shape, tv), decay, 1.0)
        k = jnp.where(_row_valid(k.shape, tv), k, 0.0)
    side = (_col_pieces(xn, wm_ref, 0, rows_in, c_g, c_g + v_w)
            + _col_pieces(y, wo_ref, 0, d_a, 0, wo_ref.shape[1]))
    done = []
    chunks = range(0, tt, SCAN_CHUNK)
    items = [(hd, q[c0:c0 + SCAN_CHUNK, hd * dk_b:(hd + 1) * dk_b],
              k[c0:c0 + SCAN_CHUNK, hd * dk_b:(hd + 1) * dk_b],
              v[c0:c0 + SCAN_CHUNK, hd * dv_b:(hd + 1) * dv_b],
              decay[c0:c0 + SCAN_CHUNK, hd * dk_b:(hd + 1) * dk_b])
             for c0 in chunks for hd in range(n_heads_b)]
    outs = _gated_chunks(items, gla_ref, eye, split, side, done)
    _run_all(side, done)
    gate = jnp.concatenate(done[:v_w // MXU_COLS], axis=1)
    acc = jnp.concatenate(done[v_w // MXU_COLS:], axis=1)
    rows = []
    for ci, c0 in enumerate(chunks):
        rows.append(jnp.concatenate(
            [_head_norm_gate(outs[ci * n_heads_b + hd], gnw_ref[...],
                             gate[c0:c0 + SCAN_CHUNK, hd * dv_b:(hd + 1) * dv_b])
             for hd in range(n_heads_b)], axis=1))
    o = jnp.concatenate(rows, axis=0) if len(rows) > 1 else rows[0]
    acc = acc + _mm(o.astype(BF16), wo_ref[d_a:d_a + v_w, :])
    ho_ref[...] = h + acc[:tt_in]

    @pl.when(t == last_t)
    def _():
        for hp in range(n_pairs):
            sl = slice(hp * 2 * P_A, (hp + 1) * 2 * P_A)
            ssm_o_ref[sl, :] = ssm_ref[:, sl].T
        for hd in range(n_heads_b):
            gla_o_ref[hd] = gla_ref[hd].T


def _tiles(t):
    if t % MIXER_TILE == 0:
        return MIXER_TILE, MIXER_TILE, MIXER_TILE
    assert t <= SCAN_CHUNK and t % (2 * SUBLANES) == 0
    return t, SCAN_CHUNK, t


def _batch_block(shape):
    nd = len(shape)
    return pl.BlockSpec((None,) + tuple(shape[1:]), lambda b, t: (b,) + (0,) * (nd - 1))


def _mixer_ab(h, w, states):
    bsz, t, d = h.shape
    tt_in, tt, tv = _tiles(t)
    has_state = states is not None
    d_a = w["dexp"].shape[1]
    conv_dim = w["conv_w"].shape[1]
    n_heads_b, dk_b, dv_b = w["gla_shape"]
    consts = [w["norm"], w["w_main"], w["w_small"], w["conv_w"], w["conv_b"], w["dt_bias"],
              w["a_log"], w["dexp"], w["ssm_norm"], w["w_gk"], w["b_gk"], w["gla_norm"],
              w["w_out"], w["rexp_cum"], w["rexp_dt"]]
    args = [h] + consts
    in_specs = [pl.BlockSpec((None, tt_in, d), lambda b, i: (b, i, 0))]
    in_specs += [_resident(c.shape) for c in consts]
    if has_state:
        ssm0, conv0, gla0 = states
        ssm0 = ssm0.reshape(bsz, d_a, N_A)
        args += [ssm0, conv0, gla0]
        in_specs += [_batch_block(ssm0.shape), _batch_block(conv0.shape), _batch_block(gla0.shape)]
    out_shape = [jax.ShapeDtypeStruct((bsz, t, d), F32),
                 jax.ShapeDtypeStruct((bsz, d_a, N_A), F32),
                 jax.ShapeDtypeStruct((bsz, CONV_K - 1, conv_dim), F32),
                 jax.ShapeDtypeStruct((bsz, n_heads_b, dk_b, dv_b), F32)]
    out_specs = [pl.BlockSpec((None, tt_in, d), lambda b, i: (b, i, 0))]
    out_specs += [_batch_block(s.shape) for s in out_shape[1:]]
    ho, ssm, conv, gla = pl.pallas_call(
        functools.partial(_mixer_ab_kernel, tt_in=tt_in, tt=tt, tv=tv, has_state=has_state),
        grid=(bsz, t // tt_in),
        in_specs=in_specs,
        out_specs=out_specs,
        out_shape=out_shape,
        scratch_shapes=[pltpu.VMEM((SUBLANES + tt, conv_dim), F32),
                        pltpu.VMEM((N_A, d_a), F32),
                        pltpu.VMEM((n_heads_b, dv_b, dk_b), F32)],
        compiler_params=pltpu.CompilerParams(dimension_semantics=("parallel", "arbitrary"),
                                             vmem_limit_bytes=VMEM_LIMIT),
        name="mixer_ab_state" if has_state else "mixer_ab",
    )(*args)
    return ho, ssm.reshape(bsz, d_a // P_A, P_A, N_A), conv, gla


def _mixer_c_kernel(*refs, tt_in, tt, tv, layer, has_state):
    h_ref, nw_ref, wi_ref, lbl_ref, hnw_ref, wo_ref = refs[:6]
    refs = refs[6:]
    if has_state:
        st0_ref = refs[0]
        refs = refs[1:]
    ho_ref, st_o_ref, st_ref = refs
    n_heads, dv, dk = st_ref.shape
    d_c = n_heads * dk
    v_w = n_heads * dv
    t = pl.program_id(1)
    last_t = pl.num_programs(1) - 1

    @pl.when(t == 0)
    def _():
        if has_state:
            for hd in range(n_heads):
                st_ref[hd] = st0_ref[hd].T
        else:
            st_ref[...] = jnp.zeros_like(st_ref)

    logits = lbl_ref[...]
    e = jnp.exp(logits - jnp.max(logits, axis=0, keepdims=True))
    sm = e / jnp.sum(e, axis=0, keepdims=True)
    lb = jnp.sum(sm[0:layer + 1, :], axis=0, keepdims=True) - sm[0:1, :]

    h = h_ref[...]
    xn = _rms(h, nw_ref[...]).astype(BF16)
    if tt_in < tt:
        xn = jnp.concatenate([xn, jnp.zeros((tt - tt_in, xn.shape[1]), BF16)], axis=0)
    _, eye, split = _chunk_masks(SCAN_CHUNK)

    rows_in = wi_ref.shape[0]
    half = n_heads // 2
    kw, vw = half * dk, half * dv
    chunks = range(0, tt, SCAN_CHUNK)

    def projections(hf):
        return (_col_pieces(xn, wi_ref, 0, rows_in, hf * kw, (hf + 1) * kw)
                + _col_pieces(xn, wi_ref, 0, rows_in, d_c + hf * kw, d_c + (hf + 1) * kw)
                + _col_pieces(xn, wi_ref, 0, rows_in, 2 * d_c + hf * vw, 2 * d_c + (hf + 1) * vw)
                + _col_pieces(xn, wi_ref, 0, rows_in, 2 * d_c + v_w + hf * vw,
                              2 * d_c + v_w + (hf + 1) * vw))

    def scan_half(hf, done, side, side_done):
        nk, nv = kw // MXU_COLS, vw // MXU_COLS
        lbh = lb[:, hf * kw:(hf + 1) * kw]
        q = _silu(jnp.concatenate(done[:nk], axis=1))
        forget = lbh + (1.0 - lbh) * jax.nn.sigmoid(jnp.concatenate(done[nk:2 * nk], axis=1))
        decay = jnp.maximum(forget, TINY)
        k = 1.0 - forget
        if tv < tt:
            decay = jnp.where(_row_valid(decay.shape, tv), decay, 1.0)
            k = jnp.where(_row_valid(k.shape, tv), k, 0.0)
        v = jnp.concatenate(done[2 * nk:2 * nk + nv], axis=1)
        gate = jnp.concatenate(done[2 * nk + nv:], axis=1)
        items = [(hf * half + hd, q[c0:c0 + SCAN_CHUNK, hd * dk:(hd + 1) * dk],
                  k[c0:c0 + SCAN_CHUNK, hd * dk:(hd + 1) * dk],
                  v[c0:c0 + SCAN_CHUNK, hd * dv:(hd + 1) * dv],
                  decay[c0:c0 + SCAN_CHUNK, hd * dk:(hd + 1) * dk])
                 for c0 in chunks for hd in range(half)]
        outs = _gated_chunks(items, st_ref, eye, split, side, side_done)
        rows = []
        for ci, c0 in enumerate(chunks):
            rows.append(jnp.concatenate(
                [_head_norm_gate(outs[ci * half + hd], hnw_ref[...],
                                 gate[c0:c0 + SCAN_CHUNK, hd * dv:(hd + 1) * dv])
                 for hd in range(half)], axis=1))
        return jnp.concatenate(rows, axis=0) if len(rows) > 1 else rows[0]

    first, second = [], []
    _run_all(projections(0), first)
    side = projections(1)
    o0 = scan_half(0, first, side, second)
    _run_all(side, second)
    o1 = scan_half(1, second, [], [])
    o = jnp.concatenate([o0, o1], axis=1)
    ho_ref[...] = h + _mm(o.astype(BF16), wo_ref[...])[:tt_in]

    @pl.when(t == last_t)
    def _():
        for hd in range(n_heads):
            st_o_ref[hd] = st_ref[hd].T


def _mixer_c(h, w, layer, state):
    bsz, t, d = h.shape
    tt_in, tt, tv = _tiles(t)
    has_state = state is not None
    n_heads, dk, dv = w["shape"]
    consts = [w["norm"], w["w_in"], w["lb_logits"], w["hgrn_norm"], w["w_out"]]
    args = [h] + consts
    in_specs = [pl.BlockSpec((None, tt_in, d), lambda b, i: (b, i, 0))]
    in_specs += [_resident(c.shape) for c in consts]
    if has_state:
        args.append(state)
        in_specs.append(_batch_block(state.shape))
    out_shape = [jax.ShapeDtypeStruct((bsz, t, d), F32),
                 jax.ShapeDtypeStruct((bsz, n_heads, dk, dv), F32)]
    out_specs = [pl.BlockSpec((None, tt_in, d), lambda b, i: (b, i, 0)),
                 _batch_block(out_shape[1].shape)]
    return pl.pallas_call(
        functools.partial(_mixer_c_kernel, tt_in=tt_in, tt=tt, tv=tv, layer=layer,
                          has_state=has_state),
        grid=(bsz, t // tt_in),
        in_specs=in_specs,
        out_specs=out_specs,
        out_shape=out_shape,
        scratch_shapes=[pltpu.VMEM((n_heads, dv, dk), F32)],
        compiler_params=pltpu.CompilerParams(dimension_semantics=("parallel", "arbitrary"),
                                             vmem_limit_bytes=VMEM_LIMIT),
        name="mixer_c_state" if has_state else "mixer_c",
    )(*args)


def _pad_lanes(a):
    return jnp.pad(a, [(0, 0)] * (a.ndim - 1) + [(0, LANES - a.shape[-1])])


def _prepare(norm_ffn1, ffn1_w_gu, ffn1_w_down, norm_mix, ab_w_in, ssm_conv_w, ssm_conv_b,
             ssm_dt_bias, ssm_a_log, ssm_d, ssm_norm_w, gla_w_gk, gla_b_gk, gla_norm_w, ab_w_out,
             c_w_in, hgrn_lb_logits, hgrn_norm_w, c_w_out, norm_ffn2, ffn2_w_gu, ffn2_w_down,
             norm_final):
    depth, d = norm_ffn1.shape
    n_ab = ab_w_in.shape[0]
    n_c = c_w_in.shape[0]
    d_a = ssm_norm_w.shape[1]
    n_heads_a = ssm_d.shape[1]
    conv_dim = ssm_conv_w.shape[2]
    rank = gla_w_gk.shape[1]
    qk_w = gla_w_gk.shape[2]
    dv_b = gla_norm_w.shape[1]
    v_w = ab_w_out.shape[1] - d_a
    n_heads_b = v_w // dv_b
    dk_b = qk_w // n_heads_b
    dv_c = hgrn_norm_w.shape[1]
    d_c = hgrn_lb_logits.shape[1]
    n_heads_c = c_w_out.shape[1] // dv_c
    dk_c = d_c // n_heads_c
    copies = CUM_GROUPS + DT_GROUPS
    assert n_heads_a == HEAD_GROUP and copies * HEAD_GROUP + rank <= LANES

    ffn1 = [(norm_ffn1[l][None], ffn1_w_gu[l].astype(BF16), ffn1_w_down[l].astype(BF16))
            for l in range(depth)]
    ffn2 = [(norm_ffn2[l][None], ffn2_w_gu[l].astype(BF16), ffn2_w_down[l].astype(BF16))
            for l in range(depth)]
    lane_head = jnp.arange(LANES) % HEAD_GROUP
    lane_grp = jnp.arange(LANES) // HEAD_GROUP
    rexp_cum = ((lane_head[:, None] == (jnp.arange(n_heads_a * LANES) // LANES)[None, :])
                & (lane_grp < CUM_GROUPS)[:, None]).astype(BF16)
    rexp_dt = ((lane_head[:, None] == (jnp.arange(d_a) // P_A)[None, :])
               & ((lane_grp >= CUM_GROUPS) & (lane_grp < copies))[:, None]).astype(BF16)
    glr0 = LANES - rank
    o_xbc, o_dt = d_a, d_a + conv_dim
    o_q = o_dt + n_heads_a
    o_glr = o_q + 2 * qk_w + 2 * v_w
    ab = []
    for j in range(n_ab):
        w_in = ab_w_in[j]
        w_main = jnp.concatenate([w_in[:, :o_dt], w_in[:, o_q:o_glr]], axis=1).astype(BF16)
        w_small = jnp.concatenate(
            [jnp.tile(w_in[:, o_dt:o_q], (1, copies)),
             jnp.zeros((d, glr0 - copies * HEAD_GROUP), F32), w_in[:, o_glr:]], axis=1)
        w_gk = jnp.zeros((LANES, qk_w), F32).at[glr0:].set(gla_w_gk[j])
        ab.append(dict(
            norm=norm_mix[2 * j][None], w_main=w_main, w_small=w_small.astype(BF16),
            conv_w=ssm_conv_w[j], conv_b=ssm_conv_b[j][None],
            dt_bias=_pad_lanes(jnp.tile(ssm_dt_bias[j], copies)[None]),
            a_log=_pad_lanes(jnp.tile(ssm_a_log[j], CUM_GROUPS)[None]),
            dexp=jnp.repeat(ssm_d[j], P_A)[None], ssm_norm=ssm_norm_w[j][None],
            w_gk=w_gk.astype(BF16), b_gk=gla_b_gk[j][None], gla_norm=gla_norm_w[j][None],
            w_out=ab_w_out[j].astype(BF16), rexp_cum=rexp_cum, rexp_dt=rexp_dt,
            gla_shape=(n_heads_b, dk_b, dv_b)))
    cc = []
    for j in range(n_c):
        cc.append(dict(norm=norm_mix[2 * j + 1][None], w_in=c_w_in[j].astype(BF16),
                       lb_logits=hgrn_lb_logits, hgrn_norm=hgrn_norm_w[j][None],
                       w_out=c_w_out[j].astype(BF16), shape=(n_heads_c, dk_c, dv_c)))
    return dict(ffn1=ffn1, ffn2=ffn2, ab=ab, cc=cc, final=norm_final[None], depth=depth)


def _trunk(x, states, w):
    bsz, t, d = x.shape
    depth = w["depth"]
    h = x
    new_ssm, new_conv, new_gla, new_hgrn = [], [], [], []
    for l in range(depth):
        h = _ffn(h.reshape(bsz * t, d), *w["ffn1"][l]).reshape(bsz, t, d)
        j = l // 2
        if l % 2 == 0:
            st = None if states is None else (states[0][j], states[1][j], states[2][j])
            h, s_ssm, s_conv, s_gla = _mixer_ab(h, w["ab"][j], st)
            new_ssm.append(s_ssm)
            new_conv.append(s_conv)
            new_gla.append(s_gla)
        else:
            st = None if states is None else states[3][j]
            h, s_h = _mixer_c(h, w["cc"][j], j, st)
            new_hgrn.append(s_h)
        final_w = w["final"] if l == depth - 1 else None
        h = _ffn(h.reshape(bsz * t, d), *w["ffn2"][l], final_w=final_w).reshape(bsz, t, d)
    return h, jnp.stack(new_ssm), jnp.stack(new_conv), jnp.stack(new_gla), jnp.stack(new_hgrn)


def kernel(x_prompt, x_sample, state_ssm, state_conv, state_gla, state_hgrn, norm_ffn1, ffn1_w_gu, ffn1_w_down, norm_mix, ab_w_in, ssm_conv_w, ssm_conv_b, ssm_dt_bias, ssm_a_log, ssm_d, ssm_norm_w, gla_w_gk, gla_b_gk, gla_norm_w, ab_w_out, c_w_in, hgrn_lb_logits, hgrn_norm_w, c_w_out, norm_ffn2, ffn2_w_gu, ffn2_w_down, norm_final):
    w = _prepare(norm_ffn1, ffn1_w_gu, ffn1_w_down, norm_mix, ab_w_in, ssm_conv_w, ssm_conv_b,
                 ssm_dt_bias, ssm_a_log, ssm_d, ssm_norm_w, gla_w_gk, gla_b_gk, gla_norm_w,
                 ab_w_out, c_w_in, hgrn_lb_logits, hgrn_norm_w, c_w_out, norm_ffn2, ffn2_w_gu,
                 ffn2_w_down, norm_final)
    y_p, ssm_p, conv_p, gla_p, hgrn_p = _trunk(x_prompt, None, w)
    y_s, ssm_s, conv_s, gla_s, hgrn_s = _trunk(
        x_sample, (state_ssm, state_conv, state_gla, state_hgrn), w)
    return (y_p, y_s, ssm_p, conv_p, gla_p, hgrn_p, ssm_s, conv_s, gla_s, hgrn_s)
```

```python
import functools

import jax
import jax.numpy as jnp
from jax import lax
from jax.experimental import pallas as pl
from jax.experimental.pallas import tpu as pltpu

F32 = jnp.float32
BF16 = jnp.bfloat16

EPS = 1e-6
TINY = 1e-30
CONV_K = 4
P_A = 64
N_A = 128
G_A = 2
GLA_GATE_NORM = 16.0
LOG2_E = 1.4426950408889634
LANES = 128
SUBLANES = 8
HEAD_GROUP = 16
CUM_GROUPS = 3
DT_GROUPS = 2
HIST_ROW = SUBLANES - (CONV_K - 1)

MXU_COLS = 256
SCAN_CHUNK = 128
MIXER_TILE = 512
FFN_TILE = 512
FFN_COLS = 512
VMEM_LIMIT = 56 * 1024 * 1024


def _mm(a, b):
    return jnp.dot(a, b, preferred_element_type=F32)


def _mm_nt(a, b):
    return lax.dot_general(a, b, (((1,), (1,)), ((), ())), preferred_element_type=F32)


def _mm_tn(a, b):
    return lax.dot_general(a, b, (((0,), (0,)), ((), ())), preferred_element_type=F32)


def _rms(x, w):
    return x * lax.rsqrt(jnp.mean(x * x, axis=-1, keepdims=True) + EPS) * w


def _silu(x):
    return x * jax.nn.sigmoid(x)


def _softplus(x):
    return jnp.maximum(x, 0.0) + jnp.log1p(jnp.exp(-jnp.abs(x)))


def _resident(stacked, layer):
    nd = stacked.ndim
    return pl.BlockSpec((None,) + stacked.shape[1:], lambda *_: (layer,) + (0,) * (nd - 1),
                        pipeline_mode=pl.Buffered(1))


def _ffn_kernel(*refs, d_ff, final_norm):
    if final_norm:
        x_ref, nw_ref, wgu_ref, wd_ref, fw_ref, o_ref, act_ref = refs
    else:
        x_ref, nw_ref, wgu_ref, wd_ref, o_ref, act_ref = refs
    x = x_ref[...]
    xn = _rms(x, nw_ref[...]).astype(BF16)
    for c0 in range(0, d_ff, FFN_COLS):
        cw = min(FFN_COLS, d_ff - c0)
        gate = _mm(xn, wgu_ref[:, c0:c0 + cw])
        up = _mm(xn, wgu_ref[:, d_ff + c0:d_ff + c0 + cw])
        act_ref[:, c0:c0 + cw] = (_silu(gate) * up).astype(BF16)
    y = x + 0.5 * _mm(act_ref[...], wd_ref[...])
    if final_norm:
        y = _rms(y, fw_ref[...])
    o_ref[...] = y


def _ffn(x2d, nw, wgu, wd, layer, final_w=None):
    m, d = x2d.shape
    d_ff = wd.shape[1]
    tm = min(FFN_TILE, m)
    assert m % tm == 0
    final_norm = final_w is not None
    in_specs = [pl.BlockSpec((tm, d), lambda i: (i, 0)), _resident(nw, layer),
                _resident(wgu, layer), _resident(wd, layer)]
    args = [x2d, nw, wgu, wd]
    if final_norm:
        in_specs.append(_resident(final_w, 0))
        args.append(final_w)
    return pl.pallas_call(
        functools.partial(_ffn_kernel, d_ff=d_ff, final_norm=final_norm),
        grid=(m // tm,),
        in_specs=in_specs,
        out_specs=pl.BlockSpec((tm, d), lambda i: (i, 0)),
        out_shape=jax.ShapeDtypeStruct((m, d), F32),
        scratch_shapes=[pltpu.VMEM((tm, d_ff), BF16)],
        compiler_params=pltpu.CompilerParams(dimension_semantics=("parallel",),
                                             vmem_limit_bytes=VMEM_LIMIT),
        name="ffn_final" if final_norm else "ffn",
    )(*args)


def _chunk_masks(length):
    i = lax.broadcasted_iota(jnp.int32, (length, length), 0)
    j = lax.broadcasted_iota(jnp.int32, (length, length), 1)
    return i >= j, i == j, jnp.where(i > j, i ^ j, 0)


def _merge_blocks(pre, suf, s):
    length, kdim = pre.shape
    tot = pre if suf is None else pre * suf
    if suf is None:
        suf = jnp.ones_like(pre)
    if s >= SUBLANES:
        pres, sufs = [], []
        for b0 in range(0, length, 2 * s):
            lo, up = slice(b0, b0 + s), slice(b0 + s, b0 + 2 * s)
            pres += [pre[lo], pre[up] * tot[lo]]
            sufs += [suf[lo] * tot[up], suf[up]]
        return jnp.concatenate(pres, axis=0), jnp.concatenate(sufs, axis=0)
    tiles = tot.reshape(length // SUBLANES, SUBLANES, kdim)
    below = pltpu.roll(tiles, s, 1).reshape(length, kdim)
    above = pltpu.roll(tiles, SUBLANES - s, 1).reshape(length, kdim)
    upper = (lax.broadcasted_iota(jnp.int32, (length, kdim), 0) & s) != 0
    return pre * jnp.where(upper, below, 1.0), suf * jnp.where(upper, 1.0, above)


def _col_pieces(x, w_ref, row0, row1, c0, c1):
    return [functools.partial(lambda a, b: _mm(x, w_ref[row0:row1, a:b]), c, min(c + MXU_COLS, c1))
            for c in range(c0, c1, MXU_COLS)]


def _run_one(side, done):
    if side:
        done.append(side.pop(0)())


def _run_all(side, done):
    while side:
        _run_one(side, done)


def _gated_chunks(items, st_ref, eye, split, side, done):
    length = items[0][1].shape[0]
    att = [jnp.where(eye, _mm_nt(q.astype(BF16), k.astype(BF16)), 0.0) for _, q, k, _, _ in items]
    pre = [g for _, _, _, _, g in items]
    suf = [None] * len(items)
    s = 1
    while s < length:
        _run_one(side, done)
        for i, (_, q, k, _, _) in enumerate(items):
            ks = k if suf[i] is None else k * suf[i]
            att[i] = jnp.where(split >= s, _mm_nt((q * pre[i]).astype(BF16), ks.astype(BF16)),
                               att[i])
            pre[i], suf[i] = _merge_blocks(pre[i], suf[i], s)
        s *= 2
    outs = []
    for i, (head, q, k, v, _) in enumerate(items):
        st = st_ref[head]
        outs.append(_mm(att[i].astype(BF16), v.astype(BF16))
                    + _mm_nt((q * pre[i]).astype(BF16), st.astype(BF16)))
        st_ref[head] = (st * pre[i][length - 1:length, :]
                        + _mm_tn(v.astype(BF16), (k * suf[i]).astype(BF16)))
    return outs


def _head_norm_gate(o, w, gate):
    return _rms(o, w) * _silu(gate)


def _row_valid(shape, tv):
    return lax.broadcasted_iota(jnp.int32, shape, 0) < tv


def _split_bf16(x, parts):
    out = []
    for _ in range(parts - 1):
        hi = x.astype(BF16).astype(F32)
        out.append(hi)
        x = x - hi
    out.append(x)
    return out


def _ssd_chunk(xs, bm, cm, dt, a2_row, rexp_cum, rexp_dt, st_ref, causal, side, done):
    length = xs.shape[0]
    heads = xs.shape[1] // P_A
    pair_w = 2 * P_A
    group_w = xs.shape[1] // G_A
    pairs_per_group = heads // 2 // G_A
    row = lax.broadcasted_iota(jnp.int32, (length, LANES), 0)
    grp = lax.broadcasted_iota(jnp.int32, (length, LANES), 1) // HEAD_GROUP
    cum = dt * a2_row
    sh = 1
    while sh < length:
        cum = cum + jnp.where(row >= sh, pltpu.roll(cum, sh, 0), 0.0)
        sh *= 2
    c_hi, c_mid, c_lo = _split_bf16(cum, 3)
    cum_cols = _mm(jnp.where(grp == 0, c_hi, jnp.where(grp == 1, c_mid, c_lo)).astype(BF16),
                   rexp_cum)
    d_hi, d_mid = _split_bf16(dt, 2)
    dt_x = _mm(jnp.where(grp == 3, d_hi, jnp.where(grp == 4, d_mid, 0.0)).astype(BF16), rexp_dt)
    cum_t = cum.T
    xdt = xs * dt_x
    lane = lax.broadcasted_iota(jnp.int32, (length, pair_w), 1)
    first = lane < P_A
    ys = []
    for g in range(G_A):
        bg = bm[:, g * N_A:(g + 1) * N_A].astype(BF16)
        cg = cm[:, g * N_A:(g + 1) * N_A].astype(BF16)
        cbs = jnp.where(causal, _mm_nt(cg, bg), 0.0)
        intra, from_start, to_end, end_decay = [], [], [], []
        for pr in range(pairs_per_group):
            _run_one(side, done)
            hp = g * pairs_per_group + pr
            sl = slice(hp * pair_w, (hp + 1) * pair_w)
            ms, cols = [], []
            for hh in (2 * hp, 2 * hp + 1):
                col = cum_cols[:, hh * LANES:(hh + 1) * LANES]
                cols.append(col)
                ms.append((cbs * jnp.exp2(jnp.minimum(col - cum_t[hh:hh + 1, :], 0.0))).astype(BF16))
            xp = xdt[:, sl]
            xpb = xp.astype(BF16)
            zero = jnp.zeros_like(xpb)
            intra.append(_mm(jnp.concatenate(ms, axis=1),
                             jnp.concatenate([jnp.where(first, xpb, zero),
                                              jnp.where(first, zero, xpb)], axis=0)))
            cpair = jnp.where(first, cols[0], cols[1])
            last = cpair[length - 1:length, :]
            from_start.append(jnp.exp2(cpair))
            to_end.append((xp * jnp.exp2(last - cpair)).astype(BF16))
            end_decay.append(jnp.exp2(last))
        gs = slice(g * group_w, (g + 1) * group_w)
        st = st_ref[:, gs]
        ys.append(jnp.concatenate(intra, axis=1)
                  + _mm(cg, st.astype(BF16)) * jnp.concatenate(from_start, axis=1))
        st_ref[:, gs] = (st * jnp.concatenate(end_decay, axis=1)
                         + _mm_tn(bg, jnp.concatenate(to_end, axis=1)))
    return jnp.concatenate(ys, axis=1)


def _mixer_ab_kernel(*refs, tt_in, tt, tv, has_state, n_passed):
    (h_ref, nw_ref, wm_ref, ws_ref, cw_ref, cb_ref, dtb_ref, alog_ref, dexp_ref, snw_ref,
     wgk_ref, bgk_ref, gnw_ref, wo_ref, rcum_ref, rdt_ref) = refs[:16]
    refs = refs[16:]
    if has_state:
        ssm0_ref, conv0_ref, gla0_ref = refs[:3]
        refs = refs[3:]
    refs = refs[n_passed:]
    ho_ref, ssm_o_ref, conv_o_ref, gla_o_ref, ext_ref, ssm_ref, gla_ref = refs

    d_a = dexp_ref.shape[1]
    conv_dim = cw_ref.shape[1]
    n_heads_b, dv_b, dk_b = gla_ref.shape
    qk_w = n_heads_b * dk_b
    v_w = n_heads_b * dv_b
    n_pairs = d_a // (2 * P_A)
    c_xbc, c_q, c_k, c_v, c_g = d_a, d_a + conv_dim, d_a + conv_dim + qk_w, \
        d_a + conv_dim + 2 * qk_w, d_a + conv_dim + 2 * qk_w + v_w
    t = pl.program_id(1)
    last_t = pl.num_programs(1) - 1

    @pl.when(t == 0)
    def _():
        if has_state:
            for hp in range(n_pairs):
                sl = slice(hp * 2 * P_A, (hp + 1) * 2 * P_A)
                ssm_ref[:, sl] = ssm0_ref[sl, :].T
            ext_ref[HIST_ROW:SUBLANES, :] = conv0_ref[...]
            for hd in range(n_heads_b):
                gla_ref[hd] = gla0_ref[hd].T
        else:
            ssm_ref[...] = jnp.zeros_like(ssm_ref)
            ext_ref[HIST_ROW:SUBLANES, :] = jnp.zeros((CONV_K - 1, conv_dim), F32)
            gla_ref[...] = jnp.zeros_like(gla_ref)

    h = h_ref[...]
    xn = _rms(h, nw_ref[...]).astype(BF16)
    if tt_in < tt:
        xn = jnp.concatenate([xn, jnp.zeros((tt - tt_in, xn.shape[1]), BF16)], axis=0)
    causal, eye, split = _chunk_masks(SCAN_CHUNK)

    ext_ref[SUBLANES:SUBLANES + tt, :] = _mm(xn, wm_ref[:, c_xbc:c_q])
    conv = cb_ref[...]
    for kk in range(CONV_K):
        conv = conv + cw_ref[kk:kk + 1, :] * ext_ref[HIST_ROW + kk:HIST_ROW + kk + tt, :]
    hist = ext_ref[HIST_ROW + tv:SUBLANES + tv, :]
    ext_ref[HIST_ROW:SUBLANES, :] = hist

    @pl.when(t == last_t)
    def _():
        conv_o_ref[...] = hist

    xbc = _silu(conv)
    xs = xbc[:, :d_a]
    bm = xbc[:, d_a:d_a + G_A * N_A]
    cm = xbc[:, d_a + G_A * N_A:]
    small = _mm(xn, ws_ref[...])
    dt = _softplus(small + dtb_ref[...])
    if tv < tt:
        dt = jnp.where(_row_valid(dt.shape, tv), dt, 0.0)
    lane = lax.broadcasted_iota(jnp.int32, (1, LANES), 1)
    a2_row = jnp.where(lane < CUM_GROUPS * HEAD_GROUP, -LOG2_E * jnp.exp(alog_ref[...]), 0.0)
    rows_in = wm_ref.shape[0]
    side = (_col_pieces(xn, wm_ref, 0, rows_in, c_q, c_g)
            + _col_pieces(xn, wm_ref, 0, rows_in, 0, c_xbc))
    done = []
    ys = []
    for c0 in range(0, tt, SCAN_CHUNK):
        rs = slice(c0, c0 + SCAN_CHUNK)
        ys.append(_ssd_chunk(xs[rs], bm[rs], cm[rs], dt[rs], a2_row, rcum_ref[...], rdt_ref[...],
                             ssm_ref, causal, side, done))
    _run_all(side, done)
    qkv = jnp.concatenate(done[:(c_g - c_q) // MXU_COLS], axis=1)
    z = jnp.concatenate(done[(c_g - c_q) // MXU_COLS:], axis=1)
    y = jnp.concatenate(ys, axis=0) if len(ys) > 1 else ys[0]
    y = (y + dexp_ref[...] * xs) * _silu(z)
    gw = d_a // G_A
    y = jnp.concatenate([_rms(y[:, g * gw:(g + 1) * gw], snw_ref[:, g * gw:(g + 1) * gw])
                         for g in range(G_A)], axis=1).astype(BF16)

    q = qkv[:, :qk_w] * dk_b ** -0.5
    k = qkv[:, qk_w:2 * qk_w]
    v = qkv[:, 2 * qk_w:]
    gk = _mm(small.astype(BF16), wgk_ref[...]) + bgk_ref[...]
    decay = jnp.exp(-_softplus(-gk) / GLA_GATE_NORM)
    if tv < tt:
        decay = jnp.where(_row_valid(decay.shape, tv), decay, 1.0)
        k = jnp.where(_row_valid(k.shape, tv), k, 0.0)
    side = (_col_pieces(xn, wm_ref, 0, rows_in, c_g, c_g + v_w)
            + _col_pieces(y, wo_ref, 0, d_a, 0, wo_ref.shape[1]))
    done = []
    chunks = range(0, tt, SCAN_CHUNK)
    items = [(hd, q[c0:c0 + SCAN_CHUNK, hd * dk_b:(hd + 1) * dk_b],
              k[c0:c0 + SCAN_CHUNK, hd * dk_b:(hd + 1) * dk_b],
              v[c0:c0 + SCAN_CHUNK, hd * dv_b:(hd + 1) * dv_b],
              decay[c0:c0 + SCAN_CHUNK, hd * dk_b:(hd + 1) * dk_b])
             for c0 in chunks for hd in range(n_heads_b)]
    outs = _gated_chunks(items, gla_ref, eye, split, side, done)
    _run_all(side, done)
    gate = jnp.concatenate(done[:v_w // MXU_COLS], axis=1)
    acc = jnp.concatenate(done[v_w // MXU_COLS:], axis=1)
    rows = []
    for ci, c0 in enumerate(chunks):
        rows.append(jnp.concatenate(
            [_head_norm_gate(outs[ci * n_heads_b + hd], gnw_ref[...],
                             gate[c0:c0 + SCAN_CHUNK, hd * dv_b:(hd + 1) * dv_b])
             for hd in range(n_heads_b)], axis=1))
    o = jnp.concatenate(rows, axis=0) if len(rows) > 1 else rows[0]
    acc = acc + _mm(o.astype(BF16), wo_ref[d_a:d_a + v_w, :])
    ho_ref[...] = h + acc[:tt_in]

    @pl.when(t == last_t)
    def _():
        for hp in range(n_pairs):
            sl = slice(hp * 2 * P_A, (hp + 1) * 2 * P_A)
            ssm_o_ref[sl, :] = ssm_ref[:, sl].T
        for hd in range(n_heads_b):
            gla_o_ref[hd] = gla_ref[hd].T


def _tiles(t):
    if t % MIXER_TILE == 0:
        return MIXER_TILE, MIXER_TILE, MIXER_TILE
    assert t <= SCAN_CHUNK and t % (2 * SUBLANES) == 0
    return t, SCAN_CHUNK, t


def _state_block(shape, layer):
    nd = len(shape)
    return pl.BlockSpec((None, None) + tuple(shape[2:]),
                        lambda b, t: (layer, b) + (0,) * (nd - 2))


def _state_io(states_in, out_shapes, prev_out, layer, n_main_in):
    args, in_specs, aliases = [], [], {}
    for st in states_in:
        args.append(st)
        in_specs.append(_state_block(st.shape, layer))
    for i, buf in enumerate(prev_out or ()):
        aliases[n_main_in + len(args)] = 1 + i
        args.append(buf)
        in_specs.append(pl.BlockSpec(memory_space=pl.ANY))
    out_specs = [_state_block(s.shape, layer) for s in out_shapes]
    return args, in_specs, out_specs, aliases


def _mixer_ab(h, w, layer, states, prev_out):
    bsz, t, d = h.shape
    tt_in, tt, tv = _tiles(t)
    has_state = states is not None
    n_layers = w["w_main"].shape[0]
    d_a = w["dexp"].shape[2]
    conv_dim = w["conv_w"].shape[2]
    n_heads_b, dk_b, dv_b = w["gla_shape"]
    consts = [w["norm"], w["w_main"], w["w_small"], w["conv_w"], w["conv_b"], w["dt_bias"],
              w["a_log"], w["dexp"], w["ssm_norm"], w["w_gk"], w["b_gk"], w["gla_norm"],
              w["w_out"]]
    shared = [w["rexp_cum"], w["rexp_dt"]]
    args = [h] + consts + shared
    in_specs = [pl.BlockSpec((None, tt_in, d), lambda b, i: (b, i, 0))]
    in_specs += [_resident(c, layer) for c in consts] + [_resident(c, 0) for c in shared]
    state_shapes = [jax.ShapeDtypeStruct((n_layers, bsz, d_a, N_A), F32),
                    jax.ShapeDtypeStruct((n_layers, bsz, CONV_K - 1, conv_dim), F32),
                    jax.ShapeDtypeStruct((n_layers, bsz, n_heads_b, dk_b, dv_b), F32)]
    states_in = []
    if has_state:
        ssm0, conv0, gla0 = states
        states_in = [ssm0.reshape(ssm0.shape[0], bsz, d_a, N_A), conv0, gla0]
    st_args, st_in_specs, st_out_specs, aliases = _state_io(
        states_in, state_shapes, prev_out, layer, len(args))
    return pl.pallas_call(
        functools.partial(_mixer_ab_kernel, tt_in=tt_in, tt=tt, tv=tv, has_state=has_state,
                          n_passed=len(prev_out or ())),
        grid=(bsz, t // tt_in),
        in_specs=in_specs + st_in_specs,
        out_specs=[pl.BlockSpec((None, tt_in, d), lambda b, i: (b, i, 0))] + st_out_specs,
        out_shape=[jax.ShapeDtypeStruct((bsz, t, d), F32)] + state_shapes,
        input_output_aliases=aliases,
        scratch_shapes=[pltpu.VMEM((SUBLANES + tt, conv_dim), F32),
                        pltpu.VMEM((N_A, d_a), F32),
                        pltpu.VMEM((n_heads_b, dv_b, dk_b), F32)],
        compiler_params=pltpu.CompilerParams(dimension_semantics=("parallel", "arbitrary"),
                                             vmem_limit_bytes=VMEM_LIMIT),
        name="mixer_ab_state" if has_state else "mixer_ab",
    )(*(args + st_args))


def _mixer_c_kernel(*refs, tt_in, tt, tv, layer, has_state, n_passed):
    h_ref, nw_ref, wi_ref, lbl_ref, hnw_ref, wo_ref = refs[:6]
    refs = refs[6:]
    if has_state:
        st0_ref = refs[0]
        refs = refs[1:]
    refs = refs[n_passed:]
    ho_ref, st_o_ref, st_ref = refs
    n_heads, dv, dk = st_ref.shape
    d_c = n_heads * dk
    v_w = n_heads * dv
    t = pl.program_id(1)
    last_t = pl.num_programs(1) - 1

    @pl.when(t == 0)
    def _():
        if has_state:
            for hd in range(n_heads):
                st_ref[hd] = st0_ref[hd].T
        else:
            st_ref[...] = jnp.zeros_like(st_ref)

    logits = lbl_ref[...]
    e = jnp.exp(logits - jnp.max(logits, axis=0, keepdims=True))
    sm = e / jnp.sum(e, axis=0, keepdims=True)
    lb = jnp.sum(sm[0:layer + 1, :], axis=0, keepdims=True) - sm[0:1, :]

    h = h_ref[...]
    xn = _rms(h, nw_ref[...]).astype(BF16)
    if tt_in < tt:
        xn = jnp.concatenate([xn, jnp.zeros((tt - tt_in, xn.shape[1]), BF16)], axis=0)
    _, eye, split = _chunk_masks(SCAN_CHUNK)

    rows_in = wi_ref.shape[0]
    half = n_heads // 2
    kw, vw = half * dk, half * dv
    chunks = range(0, tt, SCAN_CHUNK)

    def projections(hf):
        return (_col_pieces(xn, wi_ref, 0, rows_in, hf * kw, (hf + 1) * kw)
                + _col_pieces(xn, wi_ref, 0, rows_in, d_c + hf * kw, d_c + (hf + 1) * kw)
                + _col_pieces(xn, wi_ref, 0, rows_in, 2 * d_c + hf * vw, 2 * d_c + (hf + 1) * vw)
                + _col_pieces(xn, wi_ref, 0, rows_in, 2 * d_c + v_w + hf * vw,
                              2 * d_c + v_w + (hf + 1) * vw))

    def scan_half(hf, done, side, side_done):
        nk, nv = kw // MXU_COLS, vw // MXU_COLS
        lbh = lb[:, hf * kw:(hf + 1) * kw]
        q = _silu(jnp.concatenate(done[:nk], axis=1))
        forget = lbh + (1.0 - lbh) * jax.nn.sigmoid(jnp.concatenate(done[nk:2 * nk], axis=1))
        decay = jnp.maximum(forget, TINY)
        k = 1.0 - forget
        if tv < tt:
            decay = jnp.where(_row_valid(decay.shape, tv), decay, 1.0)
            k = jnp.where(_row_valid(k.shape, tv), k, 0.0)
        v = jnp.concatenate(done[2 * nk:2 * nk + nv], axis=1)
        gate = jnp.concatenate(done[2 * nk + nv:], axis=1)
        items = [(hf * half + hd, q[c0:c0 + SCAN_CHUNK, hd * dk:(hd + 1) * dk],
                  k[c0:c0 + SCAN_CHUNK, hd * dk:(hd + 1) * dk],
                  v[c0:c0 + SCAN_CHUNK, hd * dv:(hd + 1) * dv],
                  decay[c0:c0 + SCAN_CHUNK, hd * dk:(hd + 1) * dk])
                 for c0 in chunks for hd in range(half)]
        outs = _gated_chunks(items, st_ref, eye, split, side, side_done)
        rows = []
        for ci, c0 in enumerate(chunks):
            rows.append(jnp.concatenate(
                [_head_norm_gate(outs[ci * half + hd], hnw_ref[...],
                                 gate[c0:c0 + SCAN_CHUNK, hd * dv:(hd + 1) * dv])
                 for hd in range(half)], axis=1))
        return jnp.concatenate(rows, axis=0) if len(rows) > 1 else rows[0]

    first, second = [], []
    _run_all(projections(0), first)
    side = projections(1)
    o0 = scan_half(0, first, side, second)
    _run_all(side, second)
    o1 = scan_half(1, second, [], [])
    o = jnp.concatenate([o0, o1], axis=1)
    ho_ref[...] = h + _mm(o.astype(BF16), wo_ref[...])[:tt_in]

    @pl.when(t == last_t)
    def _():
        for hd in range(n_heads):
            st_o_ref[hd] = st_ref[hd].T


def _mixer_c(h, w, layer, state, prev_out):
    bsz, t, d = h.shape
    tt_in, tt, tv = _tiles(t)
    has_state = state is not None
    n_layers = w["w_in"].shape[0]
    n_heads, dk, dv = w["shape"]
    args = [h, w["norm"], w["w_in"], w["lb_logits"], w["hgrn_norm"], w["w_out"]]
    in_specs = [pl.BlockSpec((None, tt_in, d), lambda b, i: (b, i, 0)),
                _resident(w["norm"], layer), _resident(w["w_in"], layer),
                _resident(w["lb_logits"], 0), _resident(w["hgrn_norm"], layer),
                _resident(w["w_out"], layer)]
    state_shapes = [jax.ShapeDtypeStruct((n_layers, bsz, n_heads, dk, dv), F32)]
    st_args, st_in_specs, st_out_specs, aliases = _state_io(
        [state] if has_state else [], state_shapes, prev_out, layer, len(args))
    return pl.pallas_call(
        functools.partial(_mixer_c_kernel, tt_in=tt_in, tt=tt, tv=tv, layer=layer,
                          has_state=has_state, n_passed=len(prev_out or ())),
        grid=(bsz, t // tt_in),
        in_specs=in_specs + st_in_specs,
        out_specs=[pl.BlockSpec((None, tt_in, d), lambda b, i: (b, i, 0))] + st_out_specs,
        out_shape=[jax.ShapeDtypeStruct((bsz, t, d), F32)] + state_shapes,
        input_output_aliases=aliases,
        scratch_shapes=[pltpu.VMEM((n_heads, dv, dk), F32)],
        compiler_params=pltpu.CompilerParams(dimension_semantics=("parallel", "arbitrary"),
                                             vmem_limit_bytes=VMEM_LIMIT),
        name="mixer_c_state" if has_state else "mixer_c",
    )(*(args + st_args))


def _pad_lanes(a):
    return jnp.pad(a, [(0, 0)] * (a.ndim - 1) + [(0, LANES - a.shape[-1])])


def _prepare(norm_ffn1, ffn1_w_gu, ffn1_w_down, norm_mix, ab_w_in, ssm_conv_w, ssm_conv_b,
             ssm_dt_bias, ssm_a_log, ssm_d, ssm_norm_w, gla_w_gk, gla_b_gk, gla_norm_w, ab_w_out,
             c_w_in, hgrn_lb_logits, hgrn_norm_w, c_w_out, norm_ffn2, ffn2_w_gu, ffn2_w_down,
             norm_final):
    depth, d = norm_ffn1.shape
    n_ab = ab_w_in.shape[0]
    d_a = ssm_norm_w.shape[1]
    n_heads_a = ssm_d.shape[1]
    conv_dim = ssm_conv_w.shape[2]
    rank = gla_w_gk.shape[1]
    qk_w = gla_w_gk.shape[2]
    dv_b = gla_norm_w.shape[1]
    v_w = ab_w_out.shape[1] - d_a
    n_heads_b = v_w // dv_b
    dk_b = qk_w // n_heads_b
    dv_c = hgrn_norm_w.shape[1]
    d_c = hgrn_lb_logits.shape[1]
    n_heads_c = c_w_out.shape[1] // dv_c
    dk_c = d_c // n_heads_c
    copies = CUM_GROUPS + DT_GROUPS
    assert n_heads_a == HEAD_GROUP and copies * HEAD_GROUP + rank <= LANES

    ffn1 = (norm_ffn1[:, None], ffn1_w_gu.astype(BF16), ffn1_w_down.astype(BF16))
    ffn2 = (norm_ffn2[:, None], ffn2_w_gu.astype(BF16), ffn2_w_down.astype(BF16))
    lane_head = jnp.arange(LANES) % HEAD_GROUP
    lane_grp = jnp.arange(LANES) // HEAD_GROUP
    rexp_cum = ((lane_head[:, None] == (jnp.arange(n_heads_a * LANES) // LANES)[None, :])
                & (lane_grp < CUM_GROUPS)[:, None]).astype(BF16)
    rexp_dt = ((lane_head[:, None] == (jnp.arange(d_a) // P_A)[None, :])
               & ((lane_grp >= CUM_GROUPS) & (lane_grp < copies))[:, None]).astype(BF16)
    glr0 = LANES - rank
    o_dt = d_a + conv_dim
    o_q = o_dt + n_heads_a
    o_glr = o_q + 2 * qk_w + 2 * v_w
    w_main = jnp.concatenate([ab_w_in[:, :, :o_dt], ab_w_in[:, :, o_q:o_glr]], axis=2)
    w_small = jnp.concatenate(
        [jnp.tile(ab_w_in[:, :, o_dt:o_q], (1, 1, copies)),
         jnp.zeros((n_ab, d, glr0 - copies * HEAD_GROUP), F32), ab_w_in[:, :, o_glr:]], axis=2)
    w_gk = jnp.zeros((n_ab, LANES, qk_w), F32).at[:, glr0:].set(gla_w_gk)
    ab = dict(
        norm=norm_mix[0::2, None], w_main=w_main.astype(BF16), w_small=w_small.astype(BF16),
        conv_w=ssm_conv_w, conv_b=ssm_conv_b[:, None],
        dt_bias=_pad_lanes(jnp.tile(ssm_dt_bias, (1, copies)))[:, None],
        a_log=_pad_lanes(jnp.tile(ssm_a_log, (1, CUM_GROUPS)))[:, None],
        dexp=jnp.repeat(ssm_d, P_A, axis=1)[:, None], ssm_norm=ssm_norm_w[:, None],
        w_gk=w_gk.astype(BF16), b_gk=gla_b_gk[:, None], gla_norm=gla_norm_w[:, None],
        w_out=ab_w_out.astype(BF16), rexp_cum=rexp_cum[None], rexp_dt=rexp_dt[None],
        gla_shape=(n_heads_b, dk_b, dv_b))
    cc = dict(norm=norm_mix[1::2, None], w_in=c_w_in.astype(BF16),
              lb_logits=hgrn_lb_logits[None], hgrn_norm=hgrn_norm_w[:, None],
              w_out=c_w_out.astype(BF16), shape=(n_heads_c, dk_c, dv_c))
    return dict(ffn1=ffn1, ffn2=ffn2, ab=ab, cc=cc, final=norm_final[None, None], depth=depth)


def _trunk(x, states, w):
    bsz, t, d = x.shape
    depth = w["depth"]
    h = x
    ab_out = c_out = None
    for l in range(depth):
        h = _ffn(h.reshape(bsz * t, d), *w["ffn1"], l).reshape(bsz, t, d)
        j = l // 2
        if l % 2 == 0:
            h, *ab_out = _mixer_ab(h, w["ab"], j, None if states is None else states[:3], ab_out)
        else:
            h, *c_out = _mixer_c(h, w["cc"], j, None if states is None else states[3], c_out)
        final_w = w["final"] if l == depth - 1 else None
        h = _ffn(h.reshape(bsz * t, d), *w["ffn2"], l, final_w=final_w).reshape(bsz, t, d)
    ssm, conv, gla = ab_out
    n_ab, _, d_a, n_a = ssm.shape
    return h, ssm.reshape(n_ab, bsz, d_a // P_A, P_A, n_a), conv, gla, c_out[0]


def kernel(x_prompt, x_sample, state_ssm, state_conv, state_gla, state_hgrn, norm_ffn1, ffn1_w_gu, ffn1_w_down, norm_mix, ab_w_in, ssm_conv_w, ssm_conv_b, ssm_dt_bias, ssm_a_log, ssm_d, ssm_norm_w, gla_w_gk, gla_b_gk, gla_norm_w, ab_w_out, c_w_in, hgrn_lb_logits, hgrn_norm_w, c_w_out, norm_ffn2, ffn2_w_gu, ffn2_w_down, norm_final):
    w = _prepare(norm_ffn1, ffn1_w_gu, ffn1_w_down, norm_mix, ab_w_in, ssm_conv_w, ssm_conv_b,
                 ssm_dt_bias, ssm_a_log, ssm_d, ssm_norm_w, gla_w_gk, gla_b_gk, gla_norm_w,
                 ab_w_out, c_w_in, hgrn_lb_logits, hgrn_norm_w, c_w_out, norm_ffn2, ffn2_w_gu,
                 ffn2_w_down, norm_final)
    y_p, ssm_p, conv_p, gla_p, hgrn_p = _trunk(x_prompt, None, w)
    y_s, ssm_s, conv_s, gla_s, hgrn_s = _trunk(
        x_sample, (state_ssm, state_conv, state_gla, state_hgrn), w)
    return (y_p, y_s, ssm_p, conv_p, gla_p, hgrn_p, ssm_s, conv_s, gla_s, hgrn_s)
```

```python
import functools

import jax
import jax.numpy as jnp
from jax import lax
from jax.experimental import pallas as pl
from jax.experimental.pallas import tpu as pltpu

F32 = jnp.float32
BF16 = jnp.bfloat16

EPS = 1e-6
TINY = 1e-30
CONV_K = 4
P_A = 64
N_A = 128
G_A = 2
GLA_GATE_NORM = 16.0
LOG2_E = 1.4426950408889634
LANES = 128
SUBLANES = 8
HEAD_GROUP = 16
CUM_GROUPS = 3
DT_GROUPS = 2
HIST_ROW = SUBLANES - (CONV_K - 1)

MXU_COLS = 256
SCAN_CHUNK = 128
MIXER_TILE = 512
FFN_TILE = 1024
FFN_COLS = 512
VMEM_LIMIT = 56 * 1024 * 1024


def _mm(a, b):
    return jnp.dot(a, b, preferred_element_type=F32)


def _mm_nt(a, b):
    return lax.dot_general(a, b, (((1,), (1,)), ((), ())), preferred_element_type=F32)


def _mm_tn(a, b):
    return lax.dot_general(a, b, (((0,), (0,)), ((), ())), preferred_element_type=F32)


def _rms(x, w):
    return x * lax.rsqrt(jnp.mean(x * x, axis=-1, keepdims=True) + EPS) * w


def _silu(x):
    return x * jax.nn.sigmoid(x)


def _softplus(x):
    return jnp.maximum(x, 0.0) + jnp.log1p(jnp.exp(-jnp.abs(x)))


def _resident(stacked, layer):
    nd = stacked.ndim
    return pl.BlockSpec((None,) + stacked.shape[1:], lambda *_: (layer,) + (0,) * (nd - 1),
                        pipeline_mode=pl.Buffered(1))


def _ffn_kernel(*refs, d_ff, final_norm):
    if final_norm:
        x_ref, nw_ref, wgu_ref, wd_ref, fw_ref, o_ref, act_ref = refs
    else:
        x_ref, nw_ref, wgu_ref, wd_ref, o_ref, act_ref = refs
    x = x_ref[...]
    xn = _rms(x, nw_ref[...]).astype(BF16)
    for c0 in range(0, d_ff, FFN_COLS):
        cw = min(FFN_COLS, d_ff - c0)
        gate = _mm(xn, wgu_ref[:, c0:c0 + cw])
        up = _mm(xn, wgu_ref[:, d_ff + c0:d_ff + c0 + cw])
        act_ref[:, c0:c0 + cw] = (_silu(gate) * up).astype(BF16)
    y = x + 0.5 * _mm(act_ref[...], wd_ref[...])
    if final_norm:
        y = _rms(y, fw_ref[...])
    o_ref[...] = y


def _ffn(x2d, nw, wgu, wd, layer, final_w=None):
    m, d = x2d.shape
    d_ff = wd.shape[1]
    tm = min(FFN_TILE, m)
    assert m % tm == 0
    final_norm = final_w is not None
    in_specs = [pl.BlockSpec((tm, d), lambda i: (i, 0)), _resident(nw, layer),
                _resident(wgu, layer), _resident(wd, layer)]
    args = [x2d, nw, wgu, wd]
    if final_norm:
        in_specs.append(_resident(final_w, 0))
        args.append(final_w)
    return pl.pallas_call(
        functools.partial(_ffn_kernel, d_ff=d_ff, final_norm=final_norm),
        grid=(m // tm,),
        in_specs=in_specs,
        out_specs=pl.BlockSpec((tm, d), lambda i: (i, 0)),
        out_shape=jax.ShapeDtypeStruct((m, d), F32),
        scratch_shapes=[pltpu.VMEM((tm, d_ff), BF16)],
        compiler_params=pltpu.CompilerParams(dimension_semantics=("parallel",),
                                             vmem_limit_bytes=VMEM_LIMIT),
        name="ffn_final" if final_norm else "ffn",
    )(*args)


def _chunk_masks(length):
    i = lax.broadcasted_iota(jnp.int32, (length, length), 0)
    j = lax.broadcasted_iota(jnp.int32, (length, length), 1)
    return i >= j, i == j, jnp.where(i > j, i ^ j, 0)


def _merge_blocks(pre, suf, s):
    length, kdim = pre.shape
    tot = pre if suf is None else pre * suf
    if suf is None:
        suf = jnp.ones_like(pre)
    if s >= SUBLANES:
        pres, sufs = [], []
        for b0 in range(0, length, 2 * s):
            lo, up = slice(b0, b0 + s), slice(b0 + s, b0 + 2 * s)
            pres += [pre[lo], pre[up] * tot[lo]]
            sufs += [suf[lo] * tot[up], suf[up]]
        return jnp.concatenate(pres, axis=0), jnp.concatenate(sufs, axis=0)
    tiles = tot.reshape(length // SUBLANES, SUBLANES, kdim)
    below = pltpu.roll(tiles, s, 1).reshape(length, kdim)
    above = pltpu.roll(tiles, SUBLANES - s, 1).reshape(length, kdim)
    upper = (lax.broadcasted_iota(jnp.int32, (length, kdim), 0) & s) != 0
    return pre * jnp.where(upper, below, 1.0), suf * jnp.where(upper, 1.0, above)


def _col_pieces(x, w_ref, row0, row1, c0, c1):
    return [functools.partial(lambda a, b: _mm(x, w_ref[row0:row1, a:b]), c, min(c + MXU_COLS, c1))
            for c in range(c0, c1, MXU_COLS)]


def _run_one(side, done):
    if side:
        done.append(side.pop(0)())


def _run_all(side, done):
    while side:
        _run_one(side, done)


def _gated_chunks(items, st_ref, eye, split, side, done):
    length = items[0][1].shape[0]
    att = [jnp.where(eye, _mm_nt(q.astype(BF16), k.astype(BF16)), 0.0) for _, q, k, _, _ in items]
    pre = [g for _, _, _, _, g in items]
    suf = [None] * len(items)
    s = 1
    while s < length:
        _run_one(side, done)
        for i, (_, q, k, _, _) in enumerate(items):
            ks = k if suf[i] is None else k * suf[i]
            att[i] = jnp.where(split >= s, _mm_nt((q * pre[i]).astype(BF16), ks.astype(BF16)),
                               att[i])
            pre[i], suf[i] = _merge_blocks(pre[i], suf[i], s)
        s *= 2
    outs = []
    for i, (head, q, k, v, _) in enumerate(items):
        st = st_ref[head]
        outs.append(_mm(att[i].astype(BF16), v.astype(BF16))
                    + _mm_nt((q * pre[i]).astype(BF16), st.astype(BF16)))
        st_ref[head] = (st * pre[i][length - 1:length, :]
                        + _mm_tn(v.astype(BF16), (k * suf[i]).astype(BF16)))
    return outs


def _head_norm_gate(o, w, gate):
    return _rms(o, w) * _silu(gate)


def _row_valid(shape, tv):
    return lax.broadcasted_iota(jnp.int32, shape, 0) < tv


def _split_bf16(x, parts):
    out = []
    for _ in range(parts - 1):
        hi = x.astype(BF16).astype(F32)
        out.append(hi)
        x = x - hi
    out.append(x)
    return out


def _ssd_chunk(xs, bm, cm, dt, a2_row, rexp_cum, rexp_dt, st_ref, causal, side, done):
    length = xs.shape[0]
    heads = xs.shape[1] // P_A
    pair_w = 2 * P_A
    group_w = xs.shape[1] // G_A
    pairs_per_group = heads // 2 // G_A
    row = lax.broadcasted_iota(jnp.int32, (length, LANES), 0)
    grp = lax.broadcasted_iota(jnp.int32, (length, LANES), 1) // HEAD_GROUP
    cum = dt * a2_row
    sh = 1
    while sh < length:
        cum = cum + jnp.where(row >= sh, pltpu.roll(cum, sh, 0), 0.0)
        sh *= 2
    c_hi, c_mid, c_lo = _split_bf16(cum, 3)
    cum_cols = _mm(jnp.where(grp == 0, c_hi, jnp.where(grp == 1, c_mid, c_lo)).astype(BF16),
                   rexp_cum)
    d_hi, d_mid = _split_bf16(dt, 2)
    dt_x = _mm(jnp.where(grp == 3, d_hi, jnp.where(grp == 4, d_mid, 0.0)).astype(BF16), rexp_dt)
    cum_t = cum.T
    xdt = xs * dt_x
    lane = lax.broadcasted_iota(jnp.int32, (length, pair_w), 1)
    first = lane < P_A
    ys = []
    for g in range(G_A):
        bg = bm[:, g * N_A:(g + 1) * N_A].astype(BF16)
        cg = cm[:, g * N_A:(g + 1) * N_A].astype(BF16)
        cbs = jnp.where(causal, _mm_nt(cg, bg), 0.0)
        intra, from_start, to_end, end_decay = [], [], [], []
        for pr in range(pairs_per_group):
            _run_one(side, done)
            hp = g * pairs_per_group + pr
            sl = slice(hp * pair_w, (hp + 1) * pair_w)
            ms, cols = [], []
            for hh in (2 * hp, 2 * hp + 1):
                col = cum_cols[:, hh * LANES:(hh + 1) * LANES]
                cols.append(col)
                ms.append((cbs * jnp.exp2(jnp.minimum(col - cum_t[hh:hh + 1, :], 0.0))).astype(BF16))
            xp = xdt[:, sl]
            xpb = xp.astype(BF16)
            zero = jnp.zeros_like(xpb)
            intra.append(_mm(jnp.concatenate(ms, axis=1),
                             jnp.concatenate([jnp.where(first, xpb, zero),
                                              jnp.where(first, zero, xpb)], axis=0)))
            cpair = jnp.where(first, cols[0], cols[1])
            last = cpair[length - 1:length, :]
            from_start.append(jnp.exp2(cpair))
            to_end.append((xp * jnp.exp2(last - cpair)).astype(BF16))
            end_decay.append(jnp.exp2(last))
        gs = slice(g * group_w, (g + 1) * group_w)
        st = st_ref[:, gs]
        ys.append(jnp.concatenate(intra, axis=1)
                  + _mm(cg, st.astype(BF16)) * jnp.concatenate(from_start, axis=1))
        st_ref[:, gs] = (st * jnp.concatenate(end_decay, axis=1)
                         + _mm_tn(bg, jnp.concatenate(to_end, axis=1)))
    return jnp.concatenate(ys, axis=1)


def _mixer_ab_kernel(*refs, tt_in, tt, tv, has_state, n_passed):
    (h_ref, nw_ref, wm_ref, ws_ref, cw_ref, cb_ref, dtb_ref, alog_ref, dexp_ref, snw_ref,
     wgk_ref, bgk_ref, gnw_ref, wo_ref, rcum_ref, rdt_ref) = refs[:16]
    refs = refs[16:]
    if has_state:
        ssm0_ref, conv0_ref, gla0_ref = refs[:3]
        refs = refs[3:]
    refs = refs[n_passed:]
    ho_ref, ssm_o_ref, conv_o_ref, gla_o_ref, ext_ref, ssm_ref, gla_ref = refs

    d_a = dexp_ref.shape[1]
    conv_dim = cw_ref.shape[1]
    n_heads_b, dv_b, dk_b = gla_ref.shape
    qk_w = n_heads_b * dk_b
    v_w = n_heads_b * dv_b
    n_pairs = d_a // (2 * P_A)
    c_xbc, c_q, c_k, c_v, c_g = d_a, d_a + conv_dim, d_a + conv_dim + qk_w, \
        d_a + conv_dim + 2 * qk_w, d_a + conv_dim + 2 * qk_w + v_w
    t = pl.program_id(1)
    last_t = pl.num_programs(1) - 1

    @pl.when(t == 0)
    def _():
        if has_state:
            for hp in range(n_pairs):
                sl = slice(hp * 2 * P_A, (hp + 1) * 2 * P_A)
                ssm_ref[:, sl] = ssm0_ref[sl, :].T
            ext_ref[HIST_ROW:SUBLANES, :] = conv0_ref[...]
            for hd in range(n_heads_b):
                gla_ref[hd] = gla0_ref[hd].T
        else:
            ssm_ref[...] = jnp.zeros_like(ssm_ref)
            ext_ref[HIST_ROW:SUBLANES, :] = jnp.zeros((CONV_K - 1, conv_dim), F32)
            gla_ref[...] = jnp.zeros_like(gla_ref)

    h = h_ref[...]
    xn = _rms(h, nw_ref[...]).astype(BF16)
    if tt_in < tt:
        xn = jnp.concatenate([xn, jnp.zeros((tt - tt_in, xn.shape[1]), BF16)], axis=0)
    causal, eye, split = _chunk_masks(SCAN_CHUNK)

    ext_ref[SUBLANES:SUBLANES + tt, :] = _mm(xn, wm_ref[:, c_xbc:c_q])
    conv = cb_ref[...]
    for kk in range(CONV_K):
        conv = conv + cw_ref[kk:kk + 1, :] * ext_ref[HIST_ROW + kk:HIST_ROW + kk + tt, :]
    hist = ext_ref[HIST_ROW + tv:SUBLANES + tv, :]
    ext_ref[HIST_ROW:SUBLANES, :] = hist

    @pl.when(t == last_t)
    def _():
        conv_o_ref[...] = hist

    xbc = _silu(conv)
    xs = xbc[:, :d_a]
    bm = xbc[:, d_a:d_a + G_A * N_A]
    cm = xbc[:, d_a + G_A * N_A:]
    small = _mm(xn, ws_ref[...])
    dt = _softplus(small + dtb_ref[...])
    if tv < tt:
        dt = jnp.where(_row_valid(dt.shape, tv), dt, 0.0)
    lane = lax.broadcasted_iota(jnp.int32, (1, LANES), 1)
    a2_row = jnp.where(lane < CUM_GROUPS * HEAD_GROUP, -LOG2_E * jnp.exp(alog_ref[...]), 0.0)
    rows_in = wm_ref.shape[0]
    side = (_col_pieces(xn, wm_ref, 0, rows_in, c_q, c_g)
            + _col_pieces(xn, wm_ref, 0, rows_in, 0, c_xbc))
    done = []
    ys = []
    for c0 in range(0, tt, SCAN_CHUNK):
        rs = slice(c0, c0 + SCAN_CHUNK)
        ys.append(_ssd_chunk(xs[rs], bm[rs], cm[rs], dt[rs], a2_row, rcum_ref[...], rdt_ref[...],
                             ssm_ref, causal, side, done))
    _run_all(side, done)
    qkv = jnp.concatenate(done[:(c_g - c_q) // MXU_COLS], axis=1)
    z = jnp.concatenate(done[(c_g - c_q) // MXU_COLS:], axis=1)
    y = jnp.concatenate(ys, axis=0) if len(ys) > 1 else ys[0]
    y = (y + dexp_ref[...] * xs) * _silu(z)
    gw = d_a // G_A
    y = jnp.concatenate([_rms(y[:, g * gw:(g + 1) * gw], snw_ref[:, g * gw:(g + 1) * gw])
                         for g in range(G_A)], axis=1).astype(BF16)

    q = qkv[:, :qk_w] * dk_b ** -0.5
    k = qkv[:, qk_w:2 * qk_w]
    v = qkv[:, 2 * qk_w:]
    gk = _mm(small.astype(BF16), wgk_ref[...]) + bgk_ref[...]
    decay = jnp.exp(-_softplus(-gk) / GLA_GATE_NORM)
    if tv < tt:
        decay = jnp.where(_row_valid(decay.shape, tv), decay, 1.0)
        k = jnp.where(_row_valid(k.shape, tv), k, 0.0)
    side = (_col_pieces(xn, wm_ref, 0, rows_in, c_g, c_g + v_w)
            + _col_pieces(y, wo_ref, 0, d_a, 0, wo_ref.shape[1]))
    done = []
    chunks = range(0, tt, SCAN_CHUNK)
    items = [(hd, q[c0:c0 + SCAN_CHUNK, hd * dk_b:(hd + 1) * dk_b],
              k[c0:c0 + SCAN_CHUNK, hd * dk_b:(hd + 1) * dk_b],
              v[c0:c0 + SCAN_CHUNK, hd * dv_b:(hd + 1) * dv_b],
              decay[c0:c0 + SCAN_CHUNK, hd * dk_b:(hd + 1) * dk_b])
             for c0 in chunks for hd in range(n_heads_b)]
    outs = _gated_chunks(items, gla_ref, eye, split, side, done)
    _run_all(side, done)
    gate = jnp.concatenate(done[:v_w // MXU_COLS], axis=1)
    acc = jnp.concatenate(done[v_w // MXU_COLS:], axis=1)
    rows = []
    for ci, c0 in enumerate(chunks):
        rows.append(jnp.concatenate(
            [_head_norm_gate(outs[ci * n_heads_b + hd], gnw_ref[...],
                             gate[c0:c0 + SCAN_CHUNK, hd * dv_b:(hd + 1) * dv_b])
             for hd in range(n_heads_b)], axis=1))
    o = jnp.concatenate(rows, axis=0) if len(rows) > 1 else rows[0]
    acc = acc + _mm(o.astype(BF16), wo_ref[d_a:d_a + v_w, :])
    ho_ref[...] = h + acc[:tt_in]

    @pl.when(t == last_t)
    def _():
        for hp in range(n_pairs):
            sl = slice(hp * 2 * P_A, (hp + 1) * 2 * P_A)
            ssm_o_ref[sl, :] = ssm_ref[:, sl].T
        for hd in range(n_heads_b):
            gla_o_ref[hd] = gla_ref[hd].T


def _tiles(t):
    if t % MIXER_TILE == 0:
        return MIXER_TILE, MIXER_TILE, MIXER_TILE
    assert t <= SCAN_CHUNK and t % (2 * SUBLANES) == 0
    return t, SCAN_CHUNK, t


def _state_block(shape, layer):
    nd = len(shape)
    return pl.BlockSpec((None, None) + tuple(shape[2:]),
                        lambda b, t: (layer, b) + (0,) * (nd - 2))


def _state_io(states_in, out_shapes, prev_out, layer, n_main_in):
    args, in_specs, aliases = [], [], {}
    for st in states_in:
        args.append(st)
        in_specs.append(_state_block(st.shape, layer))
    for i, buf in enumerate(prev_out or ()):
        aliases[n_main_in + len(args)] = 1 + i
        args.append(buf)
        in_specs.append(pl.BlockSpec(memory_space=pl.ANY))
    out_specs = [_state_block(s.shape, layer) for s in out_shapes]
    return args, in_specs, out_specs, aliases


def _mixer_ab(h, w, layer, states, prev_out):
    bsz, t, d = h.shape
    tt_in, tt, tv = _tiles(t)
    has_state = states is not None
    n_layers = w["w_main"].shape[0]
    d_a = w["dexp"].shape[2]
    conv_dim = w["conv_w"].shape[2]
    n_heads_b, dk_b, dv_b = w["gla_shape"]
    consts = [w["norm"], w["w_main"], w["w_small"], w["conv_w"], w["conv_b"], w["dt_bias"],
              w["a_log"], w["dexp"], w["ssm_norm"], w["w_gk"], w["b_gk"], w["gla_norm"],
              w["w_out"]]
    shared = [w["rexp_cum"], w["rexp_dt"]]
    args = [h] + consts + shared
    in_specs = [pl.BlockSpec((None, tt_in, d), lambda b, i: (b, i, 0))]
    in_specs += [_resident(c, layer) for c in consts] + [_resident(c, 0) for c in shared]
    state_shapes = [jax.ShapeDtypeStruct((n_layers, bsz, d_a, N_A), F32),
                    jax.ShapeDtypeStruct((n_layers, bsz, CONV_K - 1, conv_dim), F32),
                    jax.ShapeDtypeStruct((n_layers, bsz, n_heads_b, dk_b, dv_b), F32)]
    states_in = []
    if has_state:
        ssm0, conv0, gla0 = states
        states_in = [ssm0.reshape(ssm0.shape[0], bsz, d_a, N_A), conv0, gla0]
    st_args, st_in_specs, st_out_specs, aliases = _state_io(
        states_in, state_shapes, prev_out, layer, len(args))
    return pl.pallas_call(
        functools.partial(_mixer_ab_kernel, tt_in=tt_in, tt=tt, tv=tv, has_state=has_state,
                          n_passed=len(prev_out or ())),
        grid=(bsz, t // tt_in),
        in_specs=in_specs + st_in_specs,
        out_specs=[pl.BlockSpec((None, tt_in, d), lambda b, i: (b, i, 0))] + st_out_specs,
        out_shape=[jax.ShapeDtypeStruct((bsz, t, d), F32)] + state_shapes,
        input_output_aliases=aliases,
        scratch_shapes=[pltpu.VMEM((SUBLANES + tt, conv_dim), F32),
                        pltpu.VMEM((N_A, d_a), F32),
                        pltpu.VMEM((n_heads_b, dv_b, dk_b), F32)],
        compiler_params=pltpu.CompilerParams(dimension_semantics=("parallel", "arbitrary"),
                                             vmem_limit_bytes=VMEM_LIMIT),
        name="mixer_ab_state" if has_state else "mixer_ab",
    )(*(args + st_args))


def _mixer_c_kernel(*refs, tt_in, tt, tv, layer, has_state, n_passed):
    h_ref, nw_ref, wi_ref, lbl_ref, hnw_ref, wo_ref = refs[:6]
    refs = refs[6:]
    if has_state:
        st0_ref = refs[0]
        refs = refs[1:]
    refs = refs[n_passed:]
    ho_ref, st_o_ref, st_ref = refs
    n_heads, dv, dk = st_ref.shape
    d_c = n_heads * dk
    v_w = n_heads * dv
    t = pl.program_id(1)
    last_t = pl.num_programs(1) - 1

    @pl.when(t == 0)
    def _():
        if has_state:
            for hd in range(n_heads):
                st_ref[hd] = st0_ref[hd].T
        else:
            st_ref[...] = jnp.zeros_like(st_ref)

    logits = lbl_ref[...]
    e = jnp.exp(logits - jnp.max(logits, axis=0, keepdims=True))
    sm = e / jnp.sum(e, axis=0, keepdims=True)
    lb = jnp.sum(sm[0:layer + 1, :], axis=0, keepdims=True) - sm[0:1, :]

    h = h_ref[...]
    xn = _rms(h, nw_ref[...]).astype(BF16)
    if tt_in < tt:
        xn = jnp.concatenate([xn, jnp.zeros((tt - tt_in, xn.shape[1]), BF16)], axis=0)
    _, eye, split = _chunk_masks(SCAN_CHUNK)

    rows_in = wi_ref.shape[0]
    half = n_heads // 2
    kw, vw = half * dk, half * dv
    chunks = range(0, tt, SCAN_CHUNK)

    def projections(hf):
        return (_col_pieces(xn, wi_ref, 0, rows_in, hf * kw, (hf + 1) * kw)
                + _col_pieces(xn, wi_ref, 0, rows_in, d_c + hf * kw, d_c + (hf + 1) * kw)
                + _col_pieces(xn, wi_ref, 0, rows_in, 2 * d_c + hf * vw, 2 * d_c + (hf + 1) * vw)
                + _col_pieces(xn, wi_ref, 0, rows_in, 2 * d_c + v_w + hf * vw,
                              2 * d_c + v_w + (hf + 1) * vw))

    def scan_half(hf, done, side, side_done):
        nk, nv = kw // MXU_COLS, vw // MXU_COLS
        lbh = lb[:, hf * kw:(hf + 1) * kw]
        q = _silu(jnp.concatenate(done[:nk], axis=1))
        forget = lbh + (1.0 - lbh) * jax.nn.sigmoid(jnp.concatenate(done[nk:2 * nk], axis=1))
        decay = jnp.maximum(forget, TINY)
        k = 1.0 - forget
        if tv < tt:
            decay = jnp.where(_row_valid(decay.shape, tv), decay, 1.0)
            k = jnp.where(_row_valid(k.shape, tv), k, 0.0)
        v = jnp.concatenate(done[2 * nk:2 * nk + nv], axis=1)
        gate = jnp.concatenate(done[2 * nk + nv:], axis=1)
        items = [(hf * half + hd, q[c0:c0 + SCAN_CHUNK, hd * dk:(hd + 1) * dk],
                  k[c0:c0 + SCAN_CHUNK, hd * dk:(hd + 1) * dk],
                  v[c0:c0 + SCAN_CHUNK, hd * dv:(hd + 1) * dv],
                  decay[c0:c0 + SCAN_CHUNK, hd * dk:(hd + 1) * dk])
                 for c0 in chunks for hd in range(half)]
        outs = _gated_chunks(items, st_ref, eye, split, side, side_done)
        rows = []
        for ci, c0 in enumerate(chunks):
            rows.append(jnp.concatenate(
                [_head_norm_gate(outs[ci * half + hd], hnw_ref[...],
                                 gate[c0:c0 + SCAN_CHUNK, hd * dv:(hd + 1) * dv])
                 for hd in range(half)], axis=1))
        return jnp.concatenate(rows, axis=0) if len(rows) > 1 else rows[0]

    first, second = [], []
    _run_all(projections(0), first)
    side = projections(1)
    o0 = scan_half(0, first, side, second)
    _run_all(side, second)
    o1 = scan_half(1, second, [], [])
    o = jnp.concatenate([o0, o1], axis=1)
    ho_ref[...] = h + _mm(o.astype(BF16), wo_ref[...])[:tt_in]

    @pl.when(t == last_t)
    def _():
        for hd in range(n_heads):
            st_o_ref[hd] = st_ref[hd].T


def _mixer_c(h, w, layer, state, prev_out):
    bsz, t, d = h.shape
    tt_in, tt, tv = _tiles(t)
    has_state = state is not None
    n_layers = w["w_in"].shape[0]
    n_heads, dk, dv = w["shape"]
    args = [h, w["norm"], w["w_in"], w["lb_logits"], w["hgrn_norm"], w["w_out"]]
    in_specs = [pl.BlockSpec((None, tt_in, d), lambda b, i: (b, i, 0)),
                _resident(w["norm"], layer), _resident(w["w_in"], layer),
                _resident(w["lb_logits"], 0), _resident(w["hgrn_norm"], layer),
                _resident(w["w_out"], layer)]
    state_shapes = [jax.ShapeDtypeStruct((n_layers, bsz, n_heads, dk, dv), F32)]
    st_args, st_in_specs, st_out_specs, aliases = _state_io(
        [state] if has_state else [], state_shapes, prev_out, layer, len(args))
    return pl.pallas_call(
        functools.partial(_mixer_c_kernel, tt_in=tt_in, tt=tt, tv=tv, layer=layer,
                          has_state=has_state, n_passed=len(prev_out or ())),
        grid=(bsz, t // tt_in),
        in_specs=in_specs + st_in_specs,
        out_specs=[pl.BlockSpec((None, tt_in, d), lambda b, i: (b, i, 0))] + st_out_specs,
        out_shape=[jax.ShapeDtypeStruct((bsz, t, d), F32)] + state_shapes,
        input_output_aliases=aliases,
        scratch_shapes=[pltpu.VMEM((n_heads, dv, dk), F32)],
        compiler_params=pltpu.CompilerParams(dimension_semantics=("parallel", "arbitrary"),
                                             vmem_limit_bytes=VMEM_LIMIT),
        name="mixer_c_state" if has_state else "mixer_c",
    )(*(args + st_args))


def _pad_lanes(a):
    return jnp.pad(a, [(0, 0)] * (a.ndim - 1) + [(0, LANES - a.shape[-1])])


def _prepare(norm_ffn1, ffn1_w_gu, ffn1_w_down, norm_mix, ab_w_in, ssm_conv_w, ssm_conv_b,
             ssm_dt_bias, ssm_a_log, ssm_d, ssm_norm_w, gla_w_gk, gla_b_gk, gla_norm_w, ab_w_out,
             c_w_in, hgrn_lb_logits, hgrn_norm_w, c_w_out, norm_ffn2, ffn2_w_gu, ffn2_w_down,
             norm_final):
    depth, d = norm_ffn1.shape
    n_ab = ab_w_in.shape[0]
    d_a = ssm_norm_w.shape[1]
    n_heads_a = ssm_d.shape[1]
    conv_dim = ssm_conv_w.shape[2]
    rank = gla_w_gk.shape[1]
    qk_w = gla_w_gk.shape[2]
    dv_b = gla_norm_w.shape[1]
    v_w = ab_w_out.shape[1] - d_a
    n_heads_b = v_w // dv_b
    dk_b = qk_w // n_heads_b
    dv_c = hgrn_norm_w.shape[1]
    d_c = hgrn_lb_logits.shape[1]
    n_heads_c = c_w_out.shape[1] // dv_c
    dk_c = d_c // n_heads_c
    copies = CUM_GROUPS + DT_GROUPS
    assert n_heads_a == HEAD_GROUP and copies * HEAD_GROUP + rank <= LANES

    ffn1 = (norm_ffn1[:, None], ffn1_w_gu.astype(BF16), ffn1_w_down.astype(BF16))
    ffn2 = (norm_ffn2[:, None], ffn2_w_gu.astype(BF16), ffn2_w_down.astype(BF16))
    lane_head = jnp.arange(LANES) % HEAD_GROUP
    lane_grp = jnp.arange(LANES) // HEAD_GROUP
    rexp_cum = ((lane_head[:, None] == (jnp.arange(n_heads_a * LANES) // LANES)[None, :])
                & (lane_grp < CUM_GROUPS)[:, None]).astype(BF16)
    rexp_dt = ((lane_head[:, None] == (jnp.arange(d_a) // P_A)[None, :])
               & ((lane_grp >= CUM_GROUPS) & (lane_grp < copies))[:, None]).astype(BF16)
    glr0 = LANES - rank
    o_dt = d_a + conv_dim
    o_q = o_dt + n_heads_a
    o_glr = o_q + 2 * qk_w + 2 * v_w
    w_main = jnp.concatenate([ab_w_in[:, :, :o_dt], ab_w_in[:, :, o_q:o_glr]], axis=2)
    w_small = jnp.concatenate(
        [jnp.tile(ab_w_in[:, :, o_dt:o_q], (1, 1, copies)),
         jnp.zeros((n_ab, d, glr0 - copies * HEAD_GROUP), F32), ab_w_in[:, :, o_glr:]], axis=2)
    w_gk = jnp.zeros((n_ab, LANES, qk_w), F32).at[:, glr0:].set(gla_w_gk)
    ab = dict(
        norm=norm_mix[0::2, None], w_main=w_main.astype(BF16), w_small=w_small.astype(BF16),
        conv_w=ssm_conv_w, conv_b=ssm_conv_b[:, None],
        dt_bias=_pad_lanes(jnp.tile(ssm_dt_bias, (1, copies)))[:, None],
        a_log=_pad_lanes(jnp.tile(ssm_a_log, (1, CUM_GROUPS)))[:, None],
        dexp=jnp.repeat(ssm_d, P_A, axis=1)[:, None], ssm_norm=ssm_norm_w[:, None],
        w_gk=w_gk.astype(BF16), b_gk=gla_b_gk[:, None], gla_norm=gla_norm_w[:, None],
        w_out=ab_w_out.astype(BF16), rexp_cum=rexp_cum[None], rexp_dt=rexp_dt[None],
        gla_shape=(n_heads_b, dk_b, dv_b))
    cc = dict(norm=norm_mix[1::2, None], w_in=c_w_in.astype(BF16),
              lb_logits=hgrn_lb_logits[None], hgrn_norm=hgrn_norm_w[:, None],
              w_out=c_w_out.astype(BF16), shape=(n_heads_c, dk_c, dv_c))
    return dict(ffn1=ffn1, ffn2=ffn2, ab=ab, cc=cc, final=norm_final[None, None], depth=depth)


def _trunk(x, states, w):
    bsz, t, d = x.shape
    depth = w["depth"]
    h = x
    ab_out = c_out = None
    for l in range(depth):
        h = _ffn(h.reshape(bsz * t, d), *w["ffn1"], l).reshape(bsz, t, d)
        j = l // 2
        if l % 2 == 0:
            h, *ab_out = _mixer_ab(h, w["ab"], j, None if states is None else states[:3], ab_out)
        else:
            h, *c_out = _mixer_c(h, w["cc"], j, None if states is None else states[3], c_out)
        final_w = w["final"] if l == depth - 1 else None
        h = _ffn(h.reshape(bsz * t, d), *w["ffn2"], l, final_w=final_w).reshape(bsz, t, d)
    ssm, conv, gla = ab_out
    n_ab, _, d_a, n_a = ssm.shape
    return h, ssm.reshape(n_ab, bsz, d_a // P_A, P_A, n_a), conv, gla, c_out[0]


def kernel(x_prompt, x_sample, state_ssm, state_conv, state_gla, state_hgrn, norm_ffn1, ffn1_w_gu, ffn1_w_down, norm_mix, ab_w_in, ssm_conv_w, ssm_conv_b, ssm_dt_bias, ssm_a_log, ssm_d, ssm_norm_w, gla_w_gk, gla_b_gk, gla_norm_w, ab_w_out, c_w_in, hgrn_lb_logits, hgrn_norm_w, c_w_out, norm_ffn2, ffn2_w_gu, ffn2_w_down, norm_final):
    w = _prepare(norm_ffn1, ffn1_w_gu, ffn1_w_down, norm_mix, ab_w_in, ssm_conv_w, ssm_conv_b,
                 ssm_dt_bias, ssm_a_log, ssm_d, ssm_norm_w, gla_w_gk, gla_b_gk, gla_norm_w,
                 ab_w_out, c_w_in, hgrn_lb_logits, hgrn_norm_w, c_w_out, norm_ffn2, ffn2_w_gu,
                 ffn2_w_down, norm_final)
    y_p, ssm_p, conv_p, gla_p, hgrn_p = _trunk(x_prompt, None, w)
    y_s, ssm_s, conv_s, gla_s, hgrn_s = _trunk(
        x_sample, (state_ssm, state_conv, state_gla, state_hgrn), w)
    return (y_p, y_s, ssm_p, conv_p, gla_p, hgrn_p, ssm_s, conv_s, gla_s, hgrn_s)
```

```python
import functools

import jax
import jax.numpy as jnp
from jax import lax
from jax.experimental import pallas as pl
from jax.experimental.pallas import tpu as pltpu

F32 = jnp.float32
BF16 = jnp.bfloat16

EPS = 1e-6
TINY = 1e-30
CONV_K = 4
P_A = 64
N_A = 128
G_A = 2
GLA_GATE_NORM = 16.0
LOG2_E = 1.4426950408889634
LANES = 128
SUBLANES = 8
HEAD_GROUP = 16
CUM_GROUPS = 3
DT_GROUPS = 2
HIST_ROW = SUBLANES - (CONV_K - 1)

MXU_COLS = 256
SCAN_CHUNK = 128
MIXER_AB_TILE = 512
MIXER_C_TILE = 1024
FFN_TILE = 1024
FFN_COLS = 512
VMEM_LIMIT = 56 * 1024 * 1024


def _mm(a, b):
    return jnp.dot(a, b, preferred_element_type=F32)


def _mm_nt(a, b):
    return lax.dot_general(a, b, (((1,), (1,)), ((), ())), preferred_element_type=F32)


def _mm_tn(a, b):
    return lax.dot_general(a, b, (((0,), (0,)), ((), ())), preferred_element_type=F32)


def _rms(x, w):
    return x * lax.rsqrt(jnp.mean(x * x, axis=-1, keepdims=True) + EPS) * w


def _silu(x):
    return x * jax.nn.sigmoid(x)


def _softplus(x):
    return jnp.maximum(x, 0.0) + jnp.log1p(jnp.exp(-jnp.abs(x)))


def _resident(stacked, layer):
    nd = stacked.ndim
    return pl.BlockSpec((None,) + stacked.shape[1:], lambda *_: (layer,) + (0,) * (nd - 1),
                        pipeline_mode=pl.Buffered(1))


def _ffn_kernel(*refs, d_ff, final_norm):
    if final_norm:
        x_ref, nw_ref, wgu_ref, wd_ref, fw_ref, o_ref, act_ref = refs
    else:
        x_ref, nw_ref, wgu_ref, wd_ref, o_ref, act_ref = refs
    x = x_ref[...]
    xn = _rms(x, nw_ref[...]).astype(BF16)
    for c0 in range(0, d_ff, FFN_COLS):
        cw = min(FFN_COLS, d_ff - c0)
        gate = _mm(xn, wgu_ref[:, c0:c0 + cw])
        up = _mm(xn, wgu_ref[:, d_ff + c0:d_ff + c0 + cw])
        act_ref[:, c0:c0 + cw] = (_silu(gate) * up).astype(BF16)
    y = x + 0.5 * _mm(act_ref[...], wd_ref[...])
    if final_norm:
        y = _rms(y, fw_ref[...])
    o_ref[...] = y


def _ffn(x2d, nw, wgu, wd, layer, final_w=None):
    m, d = x2d.shape
    d_ff = wd.shape[1]
    tm = min(FFN_TILE, m)
    assert m % tm == 0
    final_norm = final_w is not None
    in_specs = [pl.BlockSpec((tm, d), lambda i: (i, 0)), _resident(nw, layer),
                _resident(wgu, layer), _resident(wd, layer)]
    args = [x2d, nw, wgu, wd]
    if final_norm:
        in_specs.append(_resident(final_w, 0))
        args.append(final_w)
    return pl.pallas_call(
        functools.partial(_ffn_kernel, d_ff=d_ff, final_norm=final_norm),
        grid=(m // tm,),
        in_specs=in_specs,
        out_specs=pl.BlockSpec((tm, d), lambda i: (i, 0)),
        out_shape=jax.ShapeDtypeStruct((m, d), F32),
        scratch_shapes=[pltpu.VMEM((tm, d_ff), BF16)],
        compiler_params=pltpu.CompilerParams(dimension_semantics=("parallel",),
                                             vmem_limit_bytes=VMEM_LIMIT),
        name="ffn_final" if final_norm else "ffn",
    )(*args)


def _chunk_masks(length):
    i = lax.broadcasted_iota(jnp.int32, (length, length), 0)
    j = lax.broadcasted_iota(jnp.int32, (length, length), 1)
    return i >= j, i == j, jnp.where(i > j, i ^ j, 0)


def _merge_blocks(pre, suf, s):
    length, kdim = pre.shape
    tot = pre if suf is None else pre * suf
    if suf is None:
        suf = jnp.ones_like(pre)
    if s >= SUBLANES:
        pres, sufs = [], []
        for b0 in range(0, length, 2 * s):
            lo, up = slice(b0, b0 + s), slice(b0 + s, b0 + 2 * s)
            pres += [pre[lo], pre[up] * tot[lo]]
            sufs += [suf[lo] * tot[up], suf[up]]
        return jnp.concatenate(pres, axis=0), jnp.concatenate(sufs, axis=0)
    tiles = tot.reshape(length // SUBLANES, SUBLANES, kdim)
    below = pltpu.roll(tiles, s, 1).reshape(length, kdim)
    above = pltpu.roll(tiles, SUBLANES - s, 1).reshape(length, kdim)
    upper = (lax.broadcasted_iota(jnp.int32, (length, kdim), 0) & s) != 0
    return pre * jnp.where(upper, below, 1.0), suf * jnp.where(upper, 1.0, above)


def _col_pieces(x, w_ref, row0, row1, c0, c1):
    return [functools.partial(lambda a, b: _mm(x, w_ref[row0:row1, a:b]), c, min(c + MXU_COLS, c1))
            for c in range(c0, c1, MXU_COLS)]


def _run_one(side, done):
    if side:
        done.append(side.pop(0)())


def _run_all(side, done):
    while side:
        _run_one(side, done)


def _gated_chunks(items, st_ref, eye, split, side, done):
    length = items[0][1].shape[0]
    att = [jnp.where(eye, _mm_nt(q.astype(BF16), k.astype(BF16)), 0.0) for _, q, k, _, _ in items]
    pre = [g for _, _, _, _, g in items]
    suf = [None] * len(items)
    s = 1
    while s < length:
        _run_one(side, done)
        for i, (_, q, k, _, _) in enumerate(items):
            ks = k if suf[i] is None else k * suf[i]
            att[i] = jnp.where(split >= s, _mm_nt((q * pre[i]).astype(BF16), ks.astype(BF16)),
                               att[i])
            pre[i], suf[i] = _merge_blocks(pre[i], suf[i], s)
        s *= 2
    outs = []
    for i, (head, q, k, v, _) in enumerate(items):
        st = st_ref[head]
        outs.append(_mm(att[i].astype(BF16), v.astype(BF16))
                    + _mm_nt((q * pre[i]).astype(BF16), st.astype(BF16)))
        st_ref[head] = (st * pre[i][length - 1:length, :]
                        + _mm_tn(v.astype(BF16), (k * suf[i]).astype(BF16)))
    return outs


def _head_norm_gate(o, w, gate):
    return _rms(o, w) * _silu(gate)


def _row_valid(shape, tv):
    return lax.broadcasted_iota(jnp.int32, shape, 0) < tv


def _split_bf16(x, parts):
    out = []
    for _ in range(parts - 1):
        hi = x.astype(BF16).astype(F32)
        out.append(hi)
        x = x - hi
    out.append(x)
    return out


def _ssd_chunk(xs, bm, cm, dt, a2_row, rexp_cum, rexp_dt, st_ref, causal, side, done):
    length = xs.shape[0]
    heads = xs.shape[1] // P_A
    pair_w = 2 * P_A
    group_w = xs.shape[1] // G_A
    pairs_per_group = heads // 2 // G_A
    row = lax.broadcasted_iota(jnp.int32, (length, LANES), 0)
    grp = lax.broadcasted_iota(jnp.int32, (length, LANES), 1) // HEAD_GROUP
    cum = dt * a2_row
    sh = 1
    while sh < length:
        cum = cum + jnp.where(row >= sh, pltpu.roll(cum, sh, 0), 0.0)
        sh *= 2
    c_hi, c_mid, c_lo = _split_bf16(cum, 3)
    cum_cols = _mm(jnp.where(grp == 0, c_hi, jnp.where(grp == 1, c_mid, c_lo)).astype(BF16),
                   rexp_cum)
    d_hi, d_mid = _split_bf16(dt, 2)
    dt_x = _mm(jnp.where(grp == 3, d_hi, jnp.where(grp == 4, d_mid, 0.0)).astype(BF16), rexp_dt)
    cum_t = cum.T
    xdt = xs * dt_x
    lane = lax.broadcasted_iota(jnp.int32, (length, pair_w), 1)
    first = lane < P_A
    ys = []
    for g in range(G_A):
        bg = bm[:, g * N_A:(g + 1) * N_A].astype(BF16)
        cg = cm[:, g * N_A:(g + 1) * N_A].astype(BF16)
        cbs = jnp.where(causal, _mm_nt(cg, bg), 0.0)
        intra, from_start, to_end, end_decay = [], [], [], []
        for pr in range(pairs_per_group):
            _run_one(side, done)
            hp = g * pairs_per_group + pr
            sl = slice(hp * pair_w, (hp + 1) * pair_w)
            ms, cols = [], []
            for hh in (2 * hp, 2 * hp + 1):
                col = cum_cols[:, hh * LANES:(hh + 1) * LANES]
                cols.append(col)
                ms.append((cbs * jnp.exp2(jnp.minimum(col - cum_t[hh:hh + 1, :], 0.0))).astype(BF16))
            xp = xdt[:, sl]
            xpb = xp.astype(BF16)
            zero = jnp.zeros_like(xpb)
            intra.append(_mm(jnp.concatenate(ms, axis=1),
                             jnp.concatenate([jnp.where(first, xpb, zero),
                                              jnp.where(first, zero, xpb)], axis=0)))
            cpair = jnp.where(first, cols[0], cols[1])
            last = cpair[length - 1:length, :]
            from_start.append(jnp.exp2(cpair))
            to_end.append((xp * jnp.exp2(last - cpair)).astype(BF16))
            end_decay.append(jnp.exp2(last))
        gs = slice(g * group_w, (g + 1) * group_w)
        st = st_ref[:, gs]
        ys.append(jnp.concatenate(intra, axis=1)
                  + _mm(cg, st.astype(BF16)) * jnp.concatenate(from_start, axis=1))
        st_ref[:, gs] = (st * jnp.concatenate(end_decay, axis=1)
                         + _mm_tn(bg, jnp.concatenate(to_end, axis=1)))
    return jnp.concatenate(ys, axis=1)


def _mixer_ab_kernel(*refs, tt_in, tt, tv, has_state, n_passed):
    (h_ref, nw_ref, wm_ref, ws_ref, cw_ref, cb_ref, dtb_ref, alog_ref, dexp_ref, snw_ref,
     wgk_ref, bgk_ref, gnw_ref, wo_ref, rcum_ref, rdt_ref) = refs[:16]
    refs = refs[16:]
    if has_state:
        ssm0_ref, conv0_ref, gla0_ref = refs[:3]
        refs = refs[3:]
    refs = refs[n_passed:]
    ho_ref, ssm_o_ref, conv_o_ref, gla_o_ref, ext_ref, ssm_ref, gla_ref = refs

    d_a = dexp_ref.shape[1]
    conv_dim = cw_ref.shape[1]
    n_heads_b, dv_b, dk_b = gla_ref.shape
    qk_w = n_heads_b * dk_b
    v_w = n_heads_b * dv_b
    n_pairs = d_a // (2 * P_A)
    c_xbc, c_q, c_k, c_v, c_g = d_a, d_a + conv_dim, d_a + conv_dim + qk_w, \
        d_a + conv_dim + 2 * qk_w, d_a + conv_dim + 2 * qk_w + v_w
    t = pl.program_id(1)
    last_t = pl.num_programs(1) - 1

    @pl.when(t == 0)
    def _():
        if has_state:
            for hp in range(n_pairs):
                sl = slice(hp * 2 * P_A, (hp + 1) * 2 * P_A)
                ssm_ref[:, sl] = ssm0_ref[sl, :].T
            ext_ref[HIST_ROW:SUBLANES, :] = conv0_ref[...]
            for hd in range(n_heads_b):
                gla_ref[hd] = gla0_ref[hd].T
        else:
            ssm_ref[...] = jnp.zeros_like(ssm_ref)
            ext_ref[HIST_ROW:SUBLANES, :] = jnp.zeros((CONV_K - 1, conv_dim), F32)
            gla_ref[...] = jnp.zeros_like(gla_ref)

    h = h_ref[...]
    xn = _rms(h, nw_ref[...]).astype(BF16)
    if tt_in < tt:
        xn = jnp.concatenate([xn, jnp.zeros((tt - tt_in, xn.shape[1]), BF16)], axis=0)
    causal, eye, split = _chunk_masks(SCAN_CHUNK)

    ext_ref[SUBLANES:SUBLANES + tt, :] = _mm(xn, wm_ref[:, c_xbc:c_q])
    conv = cb_ref[...]
    for kk in range(CONV_K):
        conv = conv + cw_ref[kk:kk + 1, :] * ext_ref[HIST_ROW + kk:HIST_ROW + kk + tt, :]
    hist = ext_ref[HIST_ROW + tv:SUBLANES + tv, :]
    ext_ref[HIST_ROW:SUBLANES, :] = hist

    @pl.when(t == last_t)
    def _():
        conv_o_ref[...] = hist

    xbc = _silu(conv)
    xs = xbc[:, :d_a]
    bm = xbc[:, d_a:d_a + G_A * N_A]
    cm = xbc[:, d_a + G_A * N_A:]
    small = _mm(xn, ws_ref[...])
    dt = _softplus(small + dtb_ref[...])
    if tv < tt:
        dt = jnp.where(_row_valid(dt.shape, tv), dt, 0.0)
    lane = lax.broadcasted_iota(jnp.int32, (1, LANES), 1)
    a2_row = jnp.where(lane < CUM_GROUPS * HEAD_GROUP, -LOG2_E * jnp.exp(alog_ref[...]), 0.0)
    rows_in = wm_ref.shape[0]
    side = (_col_pieces(xn, wm_ref, 0, rows_in, c_q, c_g)
            + _col_pieces(xn, wm_ref, 0, rows_in, 0, c_xbc))
    done = []
    ys = []
    for c0 in range(0, tt, SCAN_CHUNK):
        rs = slice(c0, c0 + SCAN_CHUNK)
        ys.append(_ssd_chunk(xs[rs], bm[rs], cm[rs], dt[rs], a2_row, rcum_ref[...], rdt_ref[...],
                             ssm_ref, causal, side, done))
    _run_all(side, done)
    qkv = jnp.concatenate(done[:(c_g - c_q) // MXU_COLS], axis=1)
    z = jnp.concatenate(done[(c_g - c_q) // MXU_COLS:], axis=1)
    y = jnp.concatenate(ys, axis=0) if len(ys) > 1 else ys[0]
    y = (y + dexp_ref[...] * xs) * _silu(z)
    gw = d_a // G_A
    y = jnp.concatenate([_rms(y[:, g * gw:(g + 1) * gw], snw_ref[:, g * gw:(g + 1) * gw])
                         for g in range(G_A)], axis=1).astype(BF16)

    q = qkv[:, :qk_w] * dk_b ** -0.5
    k = qkv[:, qk_w:2 * qk_w]
    v = qkv[:, 2 * qk_w:]
    gk = _mm(small.astype(BF16), wgk_ref[...]) + bgk_ref[...]
    decay = jnp.exp(-_softplus(-gk) / GLA_GATE_NORM)
    if tv < tt:
        decay = jnp.where(_row_valid(decay.shape, tv), decay, 1.0)
        k = jnp.where(_row_valid(k.shape, tv), k, 0.0)
    side = (_col_pieces(xn, wm_ref, 0, rows_in, c_g, c_g + v_w)
            + _col_pieces(y, wo_ref, 0, d_a, 0, wo_ref.shape[1]))
    done = []
    chunks = range(0, tt, SCAN_CHUNK)
    items = [(hd, q[c0:c0 + SCAN_CHUNK, hd * dk_b:(hd + 1) * dk_b],
              k[c0:c0 + SCAN_CHUNK, hd * dk_b:(hd + 1) * dk_b],
              v[c0:c0 + SCAN_CHUNK, hd * dv_b:(hd + 1) * dv_b],
              decay[c0:c0 + SCAN_CHUNK, hd * dk_b:(hd + 1) * dk_b])
             for c0 in chunks for hd in range(n_heads_b)]
    outs = _gated_chunks(items, gla_ref, eye, split, side, done)
    _run_all(side, done)
    gate = jnp.concatenate(done[:v_w // MXU_COLS], axis=1)
    acc = jnp.concatenate(done[v_w // MXU_COLS:], axis=1)
    rows = []
    for ci, c0 in enumerate(chunks):
        rows.append(jnp.concatenate(
            [_head_norm_gate(outs[ci * n_heads_b + hd], gnw_ref[...],
                             gate[c0:c0 + SCAN_CHUNK, hd * dv_b:(hd + 1) * dv_b])
             for hd in range(n_heads_b)], axis=1))
    o = jnp.concatenate(rows, axis=0) if len(rows) > 1 else rows[0]
    acc = acc + _mm(o.astype(BF16), wo_ref[d_a:d_a + v_w, :])
    ho_ref[...] = h + acc[:tt_in]

    @pl.when(t == last_t)
    def _():
        for hp in range(n_pairs):
            sl = slice(hp * 2 * P_A, (hp + 1) * 2 * P_A)
            ssm_o_ref[sl, :] = ssm_ref[:, sl].T
        for hd in range(n_heads_b):
            gla_o_ref[hd] = gla_ref[hd].T


def _tiles(t, tile):
    if t % tile == 0:
        return tile, tile, tile
    assert t <= SCAN_CHUNK and t % (2 * SUBLANES) == 0
    return t, SCAN_CHUNK, t


def _state_block(shape, layer):
    nd = len(shape)
    return pl.BlockSpec((None, None) + tuple(shape[2:]),
                        lambda b, t: (layer, b) + (0,) * (nd - 2))


def _state_io(states_in, out_shapes, prev_out, layer, n_main_in):
    args, in_specs, aliases = [], [], {}
    for st in states_in:
        args.append(st)
        in_specs.append(_state_block(st.shape, layer))
    for i, buf in enumerate(prev_out or ()):
        aliases[n_main_in + len(args)] = 1 + i
        args.append(buf)
        in_specs.append(pl.BlockSpec(memory_space=pl.ANY))
    out_specs = [_state_block(s.shape, layer) for s in out_shapes]
    return args, in_specs, out_specs, aliases


def _mixer_ab(h, w, layer, states, prev_out):
    bsz, t, d = h.shape
    tt_in, tt, tv = _tiles(t, MIXER_AB_TILE)
    has_state = states is not None
    n_layers = w["w_main"].shape[0]
    d_a = w["dexp"].shape[2]
    conv_dim = w["conv_w"].shape[2]
    n_heads_b, dk_b, dv_b = w["gla_shape"]
    consts = [w["norm"], w["w_main"], w["w_small"], w["conv_w"], w["conv_b"], w["dt_bias"],
              w["a_log"], w["dexp"], w["ssm_norm"], w["w_gk"], w["b_gk"], w["gla_norm"],
              w["w_out"]]
    shared = [w["rexp_cum"], w["rexp_dt"]]
    args = [h] + consts + shared
    in_specs = [pl.BlockSpec((None, tt_in, d), lambda b, i: (b, i, 0))]
    in_specs += [_resident(c, layer) for c in consts] + [_resident(c, 0) for c in shared]
    state_shapes = [jax.ShapeDtypeStruct((n_layers, bsz, d_a, N_A), F32),
                    jax.ShapeDtypeStruct((n_layers, bsz, CONV_K - 1, conv_dim), F32),
                    jax.ShapeDtypeStruct((n_layers, bsz, n_heads_b, dk_b, dv_b), F32)]
    states_in = []
    if has_state:
        ssm0, conv0, gla0 = states
        states_in = [ssm0.reshape(ssm0.shape[0], bsz, d_a, N_A), conv0, gla0]
    st_args, st_in_specs, st_out_specs, aliases = _state_io(
        states_in, state_shapes, prev_out, layer, len(args))
    return pl.pallas_call(
        functools.partial(_mixer_ab_kernel, tt_in=tt_in, tt=tt, tv=tv, has_state=has_state,
                          n_passed=len(prev_out or ())),
        grid=(bsz, t // tt_in),
        in_specs=in_specs + st_in_specs,
        out_specs=[pl.BlockSpec((None, tt_in, d), lambda b, i: (b, i, 0))] + st_out_specs,
        out_shape=[jax.ShapeDtypeStruct((bsz, t, d), F32)] + state_shapes,
        input_output_aliases=aliases,
        scratch_shapes=[pltpu.VMEM((SUBLANES + tt, conv_dim), F32),
                        pltpu.VMEM((N_A, d_a), F32),
                        pltpu.VMEM((n_heads_b, dv_b, dk_b), F32)],
        compiler_params=pltpu.CompilerParams(dimension_semantics=("parallel", "arbitrary"),
                                             vmem_limit_bytes=VMEM_LIMIT),
        name="mixer_ab_state" if has_state else "mixer_ab",
    )(*(args + st_args))


def _mixer_c_kernel(*refs, tt_in, tt, tv, layer, has_state, n_passed):
    h_ref, nw_ref, wi_ref, lbl_ref, hnw_ref, wo_ref = refs[:6]
    refs = refs[6:]
    if has_state:
        st0_ref = refs[0]
        refs = refs[1:]
    refs = refs[n_passed:]
    ho_ref, st_o_ref, st_ref = refs
    n_heads, dv, dk = st_ref.shape
    d_c = n_heads * dk
    v_w = n_heads * dv
    t = pl.program_id(1)
    last_t = pl.num_programs(1) - 1

    @pl.when(t == 0)
    def _():
        if has_state:
            for hd in range(n_heads):
                st_ref[hd] = st0_ref[hd].T
        else:
            st_ref[...] = jnp.zeros_like(st_ref)

    logits = lbl_ref[...]
    e = jnp.exp(logits - jnp.max(logits, axis=0, keepdims=True))
    sm = e / jnp.sum(e, axis=0, keepdims=True)
    lb = jnp.sum(sm[0:layer + 1, :], axis=0, keepdims=True) - sm[0:1, :]

    h = h_ref[...]
    xn = _rms(h, nw_ref[...]).astype(BF16)
    if tt_in < tt:
        xn = jnp.concatenate([xn, jnp.zeros((tt - tt_in, xn.shape[1]), BF16)], axis=0)
    _, eye, split = _chunk_masks(SCAN_CHUNK)

    rows_in = wi_ref.shape[0]
    half = n_heads // 2
    kw, vw = half * dk, half * dv
    chunks = range(0, tt, SCAN_CHUNK)

    def projections(hf):
        return (_col_pieces(xn, wi_ref, 0, rows_in, hf * kw, (hf + 1) * kw)
                + _col_pieces(xn, wi_ref, 0, rows_in, d_c + hf * kw, d_c + (hf + 1) * kw)
                + _col_pieces(xn, wi_ref, 0, rows_in, 2 * d_c + hf * vw, 2 * d_c + (hf + 1) * vw)
                + _col_pieces(xn, wi_ref, 0, rows_in, 2 * d_c + v_w + hf * vw,
                              2 * d_c + v_w + (hf + 1) * vw))

    def scan_half(hf, done, side, side_done):
        nk, nv = kw // MXU_COLS, vw // MXU_COLS
        lbh = lb[:, hf * kw:(hf + 1) * kw]
        q = _silu(jnp.concatenate(done[:nk], axis=1))
        forget = lbh + (1.0 - lbh) * jax.nn.sigmoid(jnp.concatenate(done[nk:2 * nk], axis=1))
        decay = jnp.maximum(forget, TINY)
        k = 1.0 - forget
        if tv < tt:
            decay = jnp.where(_row_valid(decay.shape, tv), decay, 1.0)
            k = jnp.where(_row_valid(k.shape, tv), k, 0.0)
        v = jnp.concatenate(done[2 * nk:2 * nk + nv], axis=1)
        gate = jnp.concatenate(done[2 * nk + nv:], axis=1)
        items = [(hf * half + hd, q[c0:c0 + SCAN_CHUNK, hd * dk:(hd + 1) * dk],
                  k[c0:c0 + SCAN_CHUNK, hd * dk:(hd + 1) * dk],
                  v[c0:c0 + SCAN_CHUNK, hd * dv:(hd + 1) * dv],
                  decay[c0:c0 + SCAN_CHUNK, hd * dk:(hd + 1) * dk])
                 for c0 in chunks for hd in range(half)]
        outs = _gated_chunks(items, st_ref, eye, split, side, side_done)
        rows = []
        for ci, c0 in enumerate(chunks):
            rows.append(jnp.concatenate(
                [_head_norm_gate(outs[ci * half + hd], hnw_ref[...],
                                 gate[c0:c0 + SCAN_CHUNK, hd * dv:(hd + 1) * dv])
                 for hd in range(half)], axis=1))
        return jnp.concatenate(rows, axis=0) if len(rows) > 1 else rows[0]

    first, second = [], []
    _run_all(projections(0), first)
    side = projections(1)
    o0 = scan_half(0, first, side, second)
    _run_all(side, second)
    o1 = scan_half(1, second, [], [])
    o = jnp.concatenate([o0, o1], axis=1)
    ho_ref[...] = h + _mm(o.astype(BF16), wo_ref[...])[:tt_in]

    @pl.when(t == last_t)
    def _():
        for hd in range(n_heads):
            st_o_ref[hd] = st_ref[hd].T


def _mixer_c(h, w, layer, state, prev_out):
    bsz, t, d = h.shape
    tt_in, tt, tv = _tiles(t, MIXER_C_TILE)
    has_state = state is not None
    n_layers = w["w_in"].shape[0]
    n_heads, dk, dv = w["shape"]
    args = [h, w["norm"], w["w_in"], w["lb_logits"], w["hgrn_norm"], w["w_out"]]
    in_specs = [pl.BlockSpec((None, tt_in, d), lambda b, i: (b, i, 0)),
                _resident(w["norm"], layer), _resident(w["w_in"], layer),
                _resident(w["lb_logits"], 0), _resident(w["hgrn_norm"], layer),
                _resident(w["w_out"], layer)]
    state_shapes = [jax.ShapeDtypeStruct((n_layers, bsz, n_heads, dk, dv), F32)]
    st_args, st_in_specs, st_out_specs, aliases = _state_io(
        [state] if has_state else [], state_shapes, prev_out, layer, len(args))
    return pl.pallas_call(
        functools.partial(_mixer_c_kernel, tt_in=tt_in, tt=tt, tv=tv, layer=layer,
                          has_state=has_state, n_passed=len(prev_out or ())),
        grid=(bsz, t // tt_in),
        in_specs=in_specs + st_in_specs,
        out_specs=[pl.BlockSpec((None, tt_in, d), lambda b, i: (b, i, 0))] + st_out_specs,
        out_shape=[jax.ShapeDtypeStruct((bsz, t, d), F32)] + state_shapes,
        input_output_aliases=aliases,
        scratch_shapes=[pltpu.VMEM((n_heads, dv, dk), F32)],
        compiler_params=pltpu.CompilerParams(dimension_semantics=("parallel", "arbitrary"),
                                             vmem_limit_bytes=VMEM_LIMIT),
        name="mixer_c_state" if has_state else "mixer_c",
    )(*(args + st_args))


def _pad_lanes(a):
    return jnp.pad(a, [(0, 0)] * (a.ndim - 1) + [(0, LANES - a.shape[-1])])


def _prepare(norm_ffn1, ffn1_w_gu, ffn1_w_down, norm_mix, ab_w_in, ssm_conv_w, ssm_conv_b,
             ssm_dt_bias, ssm_a_log, ssm_d, ssm_norm_w, gla_w_gk, gla_b_gk, gla_norm_w, ab_w_out,
             c_w_in, hgrn_lb_logits, hgrn_norm_w, c_w_out, norm_ffn2, ffn2_w_gu, ffn2_w_down,
             norm_final):
    depth, d = norm_ffn1.shape
    n_ab = ab_w_in.shape[0]
    d_a = ssm_norm_w.shape[1]
    n_heads_a = ssm_d.shape[1]
    conv_dim = ssm_conv_w.shape[2]
    rank = gla_w_gk.shape[1]
    qk_w = gla_w_gk.shape[2]
    dv_b = gla_norm_w.shape[1]
    v_w = ab_w_out.shape[1] - d_a
    n_heads_b = v_w // dv_b
    dk_b = qk_w // n_heads_b
    dv_c = hgrn_norm_w.shape[1]
    d_c = hgrn_lb_logits.shape[1]
    n_heads_c = c_w_out.shape[1] // dv_c
    dk_c = d_c // n_heads_c
    copies = CUM_GROUPS + DT_GROUPS
    assert n_heads_a == HEAD_GROUP and copies * HEAD_GROUP + rank <= LANES

    ffn1 = (norm_ffn1[:, None], ffn1_w_gu.astype(BF16), ffn1_w_down.astype(BF16))
    ffn2 = (norm_ffn2[:, None], ffn2_w_gu.astype(BF16), ffn2_w_down.astype(BF16))
    lane_head = jnp.arange(LANES) % HEAD_GROUP
    lane_grp = jnp.arange(LANES) // HEAD_GROUP
    rexp_cum = ((lane_head[:, None] == (jnp.arange(n_heads_a * LANES) // LANES)[None, :])
                & (lane_grp < CUM_GROUPS)[:, None]).astype(BF16)
    rexp_dt = ((lane_head[:, None] == (jnp.arange(d_a) // P_A)[None, :])
               & ((lane_grp >= CUM_GROUPS) & (lane_grp < copies))[:, None]).astype(BF16)
    glr0 = LANES - rank
    o_dt = d_a + conv_dim
    o_q = o_dt + n_heads_a
    o_glr = o_q + 2 * qk_w + 2 * v_w
    w_main = jnp.concatenate([ab_w_in[:, :, :o_dt], ab_w_in[:, :, o_q:o_glr]], axis=2)
    w_small = jnp.concatenate(
        [jnp.tile(ab_w_in[:, :, o_dt:o_q], (1, 1, copies)),
         jnp.zeros((n_ab, d, glr0 - copies * HEAD_GROUP), F32), ab_w_in[:, :, o_glr:]], axis=2)
    w_gk = jnp.zeros((n_ab, LANES, qk_w), F32).at[:, glr0:].set(gla_w_gk)
    ab = dict(
        norm=norm_mix[0::2, None], w_main=w_main.astype(BF16), w_small=w_small.astype(BF16),
        conv_w=ssm_conv_w, conv_b=ssm_conv_b[:, None],
        dt_bias=_pad_lanes(jnp.tile(ssm_dt_bias, (1, copies)))[:, None],
        a_log=_pad_lanes(jnp.tile(ssm_a_log, (1, CUM_GROUPS)))[:, None],
        dexp=jnp.repeat(ssm_d, P_A, axis=1)[:, None], ssm_norm=ssm_norm_w[:, None],
        w_gk=w_gk.astype(BF16), b_gk=gla_b_gk[:, None], gla_norm=gla_norm_w[:, None],
        w_out=ab_w_out.astype(BF16), rexp_cum=rexp_cum[None], rexp_dt=rexp_dt[None],
        gla_shape=(n_heads_b, dk_b, dv_b))
    cc = dict(norm=norm_mix[1::2, None], w_in=c_w_in.astype(BF16),
              lb_logits=hgrn_lb_logits[None], hgrn_norm=hgrn_norm_w[:, None],
              w_out=c_w_out.astype(BF16), shape=(n_heads_c, dk_c, dv_c))
    return dict(ffn1=ffn1, ffn2=ffn2, ab=ab, cc=cc, final=norm_final[None, None], depth=depth)


def _trunk(x, states, w):
    bsz, t, d = x.shape
    depth = w["depth"]
    h = x
    ab_out = c_out = None
    for l in range(depth):
        h = _ffn(h.reshape(bsz * t, d), *w["ffn1"], l).reshape(bsz, t, d)
        j = l // 2
        if l % 2 == 0:
            h, *ab_out = _mixer_ab(h, w["ab"], j, None if states is None else states[:3], ab_out)
        else:
            h, *c_out = _mixer_c(h, w["cc"], j, None if states is None else states[3], c_out)
        final_w = w["final"] if l == depth - 1 else None
        h = _ffn(h.reshape(bsz * t, d), *w["ffn2"], l, final_w=final_w).reshape(bsz, t, d)
    ssm, conv, gla = ab_out
    n_ab, _, d_a, n_a = ssm.shape
    return h, ssm.reshape(n_ab, bsz, d_a // P_A, P_A, n_a), conv, gla, c_out[0]


def kernel(x_prompt, x_sample, state_ssm, state_conv, state_gla, state_hgrn, norm_ffn1, ffn1_w_gu, ffn1_w_down, norm_mix, ab_w_in, ssm_conv_w, ssm_conv_b, ssm_dt_bias, ssm_a_log, ssm_d, ssm_norm_w, gla_w_gk, gla_b_gk, gla_norm_w, ab_w_out, c_w_in, hgrn_lb_logits, hgrn_norm_w, c_w_out, norm_ffn2, ffn2_w_gu, ffn2_w_down, norm_final):
    w = _prepare(norm_ffn1, ffn1_w_gu, ffn1_w_down, norm_mix, ab_w_in, ssm_conv_w, ssm_conv_b,
                 ssm_dt_bias, ssm_a_log, ssm_d, ssm_norm_w, gla_w_gk, gla_b_gk, gla_norm_w,
                 ab_w_out, c_w_in, hgrn_lb_logits, hgrn_norm_w, c_w_out, norm_ffn2, ffn2_w_gu,
                 ffn2_w_down, norm_final)
    y_p, ssm_p, conv_p, gla_p, hgrn_p = _trunk(x_prompt, None, w)
    y_s, ssm_s, conv_s, gla_s, hgrn_s = _trunk(
        x_sample, (state_ssm, state_conv, state_gla, state_hgrn), w)
    return (y_p, y_s, ssm_p, conv_p, gla_p, hgrn_p, ssm_s, conv_s, gla_s, hgrn_s)
```
